```python
import math
import jax, jax.numpy as jnp
from jax import lax
import numpy as np


D_MODEL = 1024
BATCH = 4
SEQ = 8192
DEPTH = 4

CTX_LEN = 256
GRID_W = 64
D_FF = 2816
MLA_HEADS = 8
MLA_NOPE = 64
MLA_ROPE = 32
MLA_V = 64
MLA_Q_RANK = 384
MLA_KV_RANK = 256
SSM_WIDTH = 512
SSM_GROUP = 16
SSM_GROUPS = SSM_WIDTH // SSM_GROUP
SSM_STATE = 64
DT_MIN = 1e-3
DT_MAX = 1e-1
GQA_HEADS = 8
GQA_KV_HEADS = 2
GQA_HEAD_DIM = 64
WINDOW = 128
BLOCK = 128
N_BRANCH = 3
N_MOD = 9
ROPE_BASE = 10000.0
EPS = 1e-6
NEG_INF = -1e30
IN_SPLITS = (MLA_Q_RANK, MLA_KV_RANK, MLA_ROPE, SSM_WIDTH, GQA_HEADS * GQA_HEAD_DIM, GQA_KV_HEADS * GQA_HEAD_DIM, GQA_KV_HEADS * GQA_HEAD_DIM, N_BRANCH * D_MODEL)
IN_DIM = sum(IN_SPLITS)

kernel_name = 'hybrid_mla_s5_swa_dit_block'


def _offsets():
    return np.cumsum(IN_SPLITS)[:-1].tolist()


def bcast(t):
    return t[..., None, :]


def rmsnorm(x, g):
    x32 = x.astype(jnp.float32)
    y = x32 * lax.rsqrt(jnp.mean(x32 * x32, axis=-1, keepdims=True) + EPS)
    return (y * g.astype(jnp.float32)).astype(x.dtype)


def modulate(x, shift, scale):
    return x * (1 + bcast(scale)) + bcast(shift)


def swiglu(x, w13, w2):
    a, b = jnp.split(x @ w13, 2, axis=-1)
    return (jax.nn.silu(a) * b) @ w2


def rope_1d(x, pos):
    n = x.shape[-1]
    inv = ROPE_BASE ** (-jnp.arange(0, n, 2, dtype=jnp.float32) / n)
    ang = pos.astype(jnp.float32)[:, None, None] * inv
    cos, sin = jnp.cos(ang), jnp.sin(ang)
    x32 = x.astype(jnp.float32)
    x1, x2 = x32[..., : n // 2], x32[..., n // 2:]
    return jnp.concatenate([x1 * cos - x2 * sin, x1 * sin + x2 * cos], axis=-1).astype(x.dtype)


def axial_rope(x, row, col):
    half = x.shape[-1] // 2
    return jnp.concatenate([rope_1d(x[..., :half], row), rope_1d(x[..., half:], col)], axis=-1)


def sink_softmax(score_list, sink_logit):
    m = sink_logit
    for s in score_list:
        m = jnp.maximum(m, s.max(axis=-1, keepdims=True))
    e = [jnp.exp(s - m) for s in score_list]
    denom = jnp.exp(sink_logit - m)
    for t in e:
        denom = denom + t.sum(axis=-1, keepdims=True)
    return [t / denom for t in e]


def dense_attention_blocks(q, k, v):
    B, T, H, dk = q.shape
    dv = v.shape[-1]
    nb = T // BLOCK
    scale = dk ** -0.5
    qb = jnp.moveaxis(q.reshape(B, nb, BLOCK, H, dk), 1, 0)

    def one_block(qi):
        s = jnp.einsum('bqhd,bkhd->bhqk', qi, k, preferred_element_type=jnp.float32) * scale
        p = jax.nn.softmax(s, axis=-1).astype(v.dtype)
        return jnp.einsum('bhqk,bkhd->bqhd', p, v)

    o = lax.map(one_block, qb)
    return jnp.moveaxis(o, 0, 1).reshape(B, T, H * dv)


def mla_queries(cq, q_norm, w_uq, row, col):
    B, T, _ = cq.shape
    q = (rmsnorm(cq, q_norm) @ w_uq).reshape(B, T, MLA_HEADS, MLA_NOPE + MLA_ROPE)
    q_nope, q_rope = q[..., :MLA_NOPE], q[..., MLA_NOPE:]
    if row is not None:
        q_rope = axial_rope(q_rope, row, col)
    return jnp.concatenate([q_nope, q_rope], axis=-1)


def mla_keys_values(ckv, kr, kv_norm, w_ukv, row, col):
    B, T, _ = ckv.shape
    kv = (rmsnorm(ckv, kv_norm) @ w_ukv).reshape(B, T, MLA_HEADS, MLA_NOPE + MLA_V)
    k_nope, v = kv[..., :MLA_NOPE], kv[..., MLA_NOPE:]
    kr = kr[:, :, None, :]
    if row is not None:
        kr = axial_rope(kr, row, col)
    k = jnp.concatenate([k_nope, jnp.broadcast_to(kr, (B, T, MLA_HEADS, MLA_ROPE))], axis=-1)
    return k, v


def window_gqa(q, k, v, kc, vc, sink):
    B, T, H, d = q.shape
    G = H // GQA_KV_HEADS
    nb = T // BLOCK
    scale = d ** -0.5
    qb = q.reshape(B, nb, BLOCK, GQA_KV_HEADS, G, d)

    def band(t):
        tb = t.reshape(B, nb, BLOCK, GQA_KV_HEADS, d)
        tp = jnp.pad(tb, ((0, 0), (1, 1), (0, 0), (0, 0), (0, 0)))
        return jnp.concatenate([tp[:, :-2], tp[:, 1:-1], tp[:, 2:]], axis=2)

    kb, vb = band(k), band(v)
    s_band = jnp.einsum('bnqhgd,bnkhd->bhgnqk', qb, kb, preferred_element_type=jnp.float32) * scale
    blk = jnp.arange(nb)[:, None, None]
    qpos = blk * BLOCK + jnp.arange(BLOCK)[None, :, None]
    kpos = (blk - 1) * BLOCK + jnp.arange(3 * BLOCK)[None, None, :]
    valid = (jnp.abs(qpos - kpos) <= WINDOW) & (kpos >= 0) & (kpos < T)
    s_band = jnp.where(valid, s_band, NEG_INF)
    s_ctx = jnp.einsum('bnqhgd,bchd->bhgnqc', qb, kc, preferred_element_type=jnp.float32) * scale
    sk = sink.astype(jnp.float32).reshape(GQA_KV_HEADS, G)[None, :, :, None, None, None]
    p_band, p_ctx = sink_softmax([s_band, s_ctx], sk)
    o = (jnp.einsum('bhgnqk,bnkhd->bnqhgd', p_band.astype(v.dtype), vb)
         + jnp.einsum('bhgnqc,bchd->bnqhgd', p_ctx.astype(vc.dtype), vc))
    return o.reshape(B, T, H * d)


def context_gqa(qc, kc, vc, sink):
    B, C, H, d = qc.shape
    G = H // GQA_KV_HEADS
    qg = qc.reshape(B, C, GQA_KV_HEADS, G, d)
    s = jnp.einsum('bqhgd,bkhd->bhgqk', qg, kc, preferred_element_type=jnp.float32) * d ** -0.5
    sk = sink.astype(jnp.float32).reshape(GQA_KV_HEADS, G)[None, :, :, None, None]
    (p,) = sink_softmax([s], sk)
    o = jnp.einsum('bhgqk,bkhd->bqhgd', p.astype(vc.dtype), vc)
    return o.reshape(B, C, H * d)


def cmul(ar, ai, br, bi):
    return ar * br - ai * bi, ar * bi + ai * br


def ssm_discretize(lam_re, lam_im, log_dt, b_re, b_im):
    lr, li = lam_re.astype(jnp.float32), lam_im.astype(jnp.float32)
    dt = jnp.exp(log_dt.astype(jnp.float32))[:, None]
    mag = jnp.exp(lr * dt)
    a_re, a_im = mag * jnp.cos(li * dt), mag * jnp.sin(li * dt)
    den = lr * lr + li * li
    w_re = ((a_re - 1) * lr + a_im * li) / den
    w_im = (a_im * lr - (a_re - 1) * li) / den
    bb_re, bb_im = cmul(w_re[..., None], w_im[..., None], b_re.astype(jnp.float32), b_im.astype(jnp.float32))
    return a_re, a_im, bb_re, bb_im


def diag_scan(a_re, a_im, b_re, b_im, h0, reverse):
    if reverse:
        b_re, b_im = jnp.flip(b_re, axis=1), jnp.flip(b_im, axis=1)
    if h0 is not None:
        i_re, i_im = cmul(a_re, a_im, h0[0], h0[1])
        b_re = b_re.at[:, 0].add(i_re)
        b_im = b_im.at[:, 0].add(i_im)
    T = b_re.shape[1]
    ar = jnp.broadcast_to(a_re, (1, T) + a_re.shape)
    ai = jnp.broadcast_to(a_im, (1, T) + a_im.shape)

    def combine(e1, e2):
        a1r, a1i, b1r, b1i = e1
        a2r, a2i, b2r, b2i = e2
        nar, nai = cmul(a2r, a2i, a1r, a1i)
        nbr, nbi = cmul(a2r, a2i, b1r, b1i)
        return nar, nai, nbr + b2r, nbi + b2i

    _, _, s_re, s_im = lax.associative_scan(combine, (ar, ai, b_re, b_im), axis=1)
    if reverse:
        s_re, s_im = jnp.flip(s_re, axis=1), jnp.flip(s_im, axis=1)
    return s_re, s_im


def ssm_branch(u_lat, u_ctx, lam_re, lam_im, log_dt, b_re, b_im, c_re, c_im, d_skip, w_glu, ctx_out):
    dtype = u_lat.dtype

    def to_groups(u):
        return u.astype(jnp.float32).reshape(u.shape[0], u.shape[1], SSM_GROUPS, SSM_GROUP)

    ul, uc = to_groups(u_lat), to_groups(u_ctx)
    d_g = d_skip.astype(jnp.float32).reshape(SSM_GROUPS, SSM_GROUP)
    y_lat = ul * d_g
    y_ctx = uc * d_g if ctx_out else None
    for direction in range(2):
        reverse = direction == 1
        a_re, a_im, bb_re, bb_im = ssm_discretize(lam_re[direction], lam_im[direction], log_dt[direction], b_re[direction], b_im[direction])
        c_r, c_i = c_re[direction].astype(jnp.float32), c_im[direction].astype(jnp.float32)

        def drive(u):
            return jnp.einsum('btgm,gpm->btgp', u, bb_re), jnp.einsum('btgm,gpm->btgp', u, bb_im)

        def readout(sr, si):
            return jnp.einsum('btgp,gmp->btgm', sr, c_r) - jnp.einsum('btgp,gmp->btgm', si, c_i)

        sc_re, sc_im = diag_scan(a_re, a_im, *drive(uc), None, reverse)
        end = 0 if reverse else -1
        sl_re, sl_im = diag_scan(a_re, a_im, *drive(ul), (sc_re[:, end], sc_im[:, end]), reverse)
        y_lat = y_lat + readout(sl_re, sl_im)
        if ctx_out:
            y_ctx = y_ctx + readout(sc_re, sc_im)

    def glu(y):
        y = jax.nn.gelu(y).reshape(y.shape[0], y.shape[1], SSM_WIDTH).astype(dtype)
        a, g = jnp.split(y @ w_glu, 2, axis=-1)
        return a * jax.nn.sigmoid(g)

    return glu(y_lat), (glu(y_ctx) if ctx_out else None)


def mixing_sublayer(xl, xc, row, col, ctx_out, w_in, mla_q_norm, mla_kv_norm, mla_w_uq, mla_w_ukv, mla_w_o,
                    lam_re, lam_im, log_dt, b_re, b_im, c_re, c_im, d_skip, w_glu, sink, gqa_w_o, w_out):
    B, T, _ = xl.shape
    C = xc.shape[1]
    offs = _offsets()
    cq, ckv, kr, u, gq, gk, gv, gates = jnp.split(xl @ w_in, offs, axis=-1)
    w_parts = jnp.split(w_in, offs, axis=1)
    ckv_c, kr_c, u_c, gk_c, gv_c = (xc @ w_parts[i] for i in (1, 2, 3, 5, 6))
    k_c, v_c = mla_keys_values(ckv_c, kr_c, mla_kv_norm, mla_w_ukv, None, None)
    gk_c = gk_c.reshape(B, C, GQA_KV_HEADS, GQA_HEAD_DIM)
    gv_c = gv_c.reshape(B, C, GQA_KV_HEADS, GQA_HEAD_DIM)
    q = mla_queries(cq, mla_q_norm, mla_w_uq, row, col)
    k, v = mla_keys_values(ckv, kr, mla_kv_norm, mla_w_ukv, row, col)
    mla_lat = dense_attention_blocks(q, jnp.concatenate([k_c, k], axis=1), jnp.concatenate([v_c, v], axis=1)) @ mla_w_o
    ssm_lat, ssm_ctx = ssm_branch(u, u_c, lam_re, lam_im, log_dt, b_re, b_im, c_re, c_im, d_skip, w_glu, ctx_out)
    gq = axial_rope(gq.reshape(B, T, GQA_HEADS, GQA_HEAD_DIM), row, col)
    gk = axial_rope(gk.reshape(B, T, GQA_KV_HEADS, GQA_HEAD_DIM), row, col)
    gv = gv.reshape(B, T, GQA_KV_HEADS, GQA_HEAD_DIM)
    gqa_lat = window_gqa(gq, gk, gv, gk_c, gv_c, sink) @ gqa_w_o

    def merge(gate_logits, b0, b1, b2):
        g0, g1, g2 = jnp.split(jax.nn.sigmoid(gate_logits), N_BRANCH, axis=-1)
        return (g0 * b0 + g1 * b1 + g2 * b2) @ w_out

    out_lat = merge(gates, mla_lat, ssm_lat, gqa_lat)
    if not ctx_out:
        return out_lat, None
    cq_c, gq_c, gates_c = (xc @ w_parts[i] for i in (0, 4, 7))
    q_c = mla_queries(cq_c, mla_q_norm, mla_w_uq, None, None)
    mla_ctx = dense_attention_blocks(q_c, k_c, v_c) @ mla_w_o
    gqa_ctx = context_gqa(gq_c.reshape(B, C, GQA_HEADS, GQA_HEAD_DIM), gk_c, gv_c, sink) @ gqa_w_o
    out_ctx = merge(gates_c, mla_ctx, ssm_ctx, gqa_ctx)
    return out_lat, out_ctx


def setup_inputs(seed: int = 0) -> dict:
    key = jax.random.key(seed)
    ks = list(jax.random.split(key, 32))
    f32 = jnp.float32

    def nrm(shape, scale):
        return jax.random.normal(ks.pop(), shape, f32) * scale

    D, F, Lr = D_MODEL, D_FF, DEPTH
    G, P, M = SSM_GROUPS, SSM_STATE, SSM_GROUP
    inp = {}
    inp['x'] = nrm((BATCH, SEQ, D), 1.0)
    inp['c'] = nrm((BATCH, D), 1.0)
    inp['ctx'] = nrm((BATCH, CTX_LEN, D), 1.0)
    inp['c_ctx'] = nrm((D,), 1.0)
    inp['ada_w'] = nrm((Lr, D, N_MOD * D), 0.5 * D ** -0.5)
    inp['ada_b'] = nrm((Lr, N_MOD * D), 0.01)
    inp['norm_ffn1'] = 1.0 + nrm((Lr, D), 0.01)
    inp['norm_mix'] = 1.0 + nrm((Lr, D), 0.01)
    inp['norm_ffn2'] = 1.0 + nrm((Lr, D), 0.01)
    inp['ffn1_w13'] = nrm((Lr, D, 2 * F), D ** -0.5)
    inp['ffn1_w2'] = nrm((Lr, F, D), F ** -0.5)
    inp['ffn2_w13'] = nrm((Lr, D, 2 * F), D ** -0.5)
    inp['ffn2_w2'] = nrm((Lr, F, D), F ** -0.5)
    inp['w_in'] = nrm((Lr, D, IN_DIM), D ** -0.5)
    inp['mla_q_norm'] = 1.0 + nrm((Lr, MLA_Q_RANK), 0.01)
    inp['mla_kv_norm'] = 1.0 + nrm((Lr, MLA_KV_RANK), 0.01)
    inp['mla_w_uq'] = nrm((Lr, MLA_Q_RANK, MLA_HEADS * (MLA_NOPE + MLA_ROPE)), MLA_Q_RANK ** -0.5)
    inp['mla_w_ukv'] = nrm((Lr, MLA_KV_RANK, MLA_HEADS * (MLA_NOPE + MLA_V)), MLA_KV_RANK ** -0.5)
    inp['mla_w_o'] = nrm((Lr, MLA_HEADS * MLA_V, D), (MLA_HEADS * MLA_V) ** -0.5)
    inp['ssm_lambda_re'] = -0.5 + nrm((Lr, 2, G, P), 0.01)
    inp['ssm_lambda_im'] = jnp.pi * jnp.arange(P, dtype=f32) + nrm((Lr, 2, G, P), 0.01)
    inp['ssm_log_dt'] = jax.random.uniform(ks.pop(), (Lr, 2, G), f32, minval=math.log(DT_MIN), maxval=math.log(DT_MAX))
    inp['ssm_b_re'] = nrm((Lr, 2, G, P, M), (2 * M) ** -0.5)
    inp['ssm_b_im'] = nrm((Lr, 2, G, P, M), (2 * M) ** -0.5)
    inp['ssm_c_re'] = nrm((Lr, 2, G, M, P), 0.5)
    inp['ssm_c_im'] = nrm((Lr, 2, G, M, P), 0.5)
    inp['ssm_d'] = nrm((Lr, SSM_WIDTH), 1.0)
    inp['ssm_w_glu'] = nrm((Lr, SSM_WIDTH, 2 * D), SSM_WIDTH ** -0.5)
    inp['gqa_sink'] = nrm((Lr, GQA_HEADS), 0.5)
    inp['gqa_w_o'] = nrm((Lr, GQA_HEADS * GQA_HEAD_DIM, D), (GQA_HEADS * GQA_HEAD_DIM) ** -0.5)
    inp['w_out'] = nrm((Lr, D, D), D ** -0.5)
    inp['final_norm'] = 1.0 + nrm((D,), 0.01)
    return inp


def reference(x, c, ctx, c_ctx, ada_w, ada_b, norm_ffn1, norm_mix, norm_ffn2, ffn1_w13, ffn1_w2, ffn2_w13, ffn2_w2,
              w_in, mla_q_norm, mla_kv_norm, mla_w_uq, mla_w_ukv, mla_w_o, ssm_lambda_re, ssm_lambda_im, ssm_log_dt,
              ssm_b_re, ssm_b_im, ssm_c_re, ssm_c_im, ssm_d, ssm_w_glu, gqa_sink, gqa_w_o, w_out, final_norm):
    L = x.shape[1]
    ROWS = L // GRID_W
    row = jnp.repeat(jnp.arange(ROWS, dtype=jnp.int32), GRID_W)
    col = jnp.tile(jnp.arange(GRID_W, dtype=jnp.int32), ROWS)
    h, hc = x, ctx
    for layer in range(DEPTH):
        ctx_out = layer < DEPTH - 1
        mod = jnp.split(jax.nn.silu(c) @ ada_w[layer] + ada_b[layer], N_MOD, axis=-1)
        mod_c = jnp.split(jax.nn.silu(c_ctx) @ ada_w[layer] + ada_b[layer], N_MOD, axis=-1)
        h = h + 0.5 * bcast(mod[2]) * swiglu(modulate(rmsnorm(h, norm_ffn1[layer]), mod[0], mod[1]), ffn1_w13[layer], ffn1_w2[layer])
        hc = hc + 0.5 * bcast(mod_c[2]) * swiglu(modulate(rmsnorm(hc, norm_ffn1[layer]), mod_c[0], mod_c[1]), ffn1_w13[layer], ffn1_w2[layer])
        mix_lat, mix_ctx = mixing_sublayer(
            modulate(rmsnorm(h, norm_mix[layer]), mod[3], mod[4]),
            modulate(rmsnorm(hc, norm_mix[layer]), mod_c[3], mod_c[4]),
            row, col, ctx_out, w_in[layer], mla_q_norm[layer], mla_kv_norm[layer], mla_w_uq[layer], mla_w_ukv[layer],
            mla_w_o[layer], ssm_lambda_re[layer], ssm_lambda_im[layer], ssm_log_dt[layer], ssm_b_re[layer], ssm_b_im[layer],
            ssm_c_re[layer], ssm_c_im[layer], ssm_d[layer], ssm_w_glu[layer], gqa_sink[layer], gqa_w_o[layer], w_out[layer])
        h = h + bcast(mod[5]) * mix_lat
        h = h + 0.5 * bcast(mod[8]) * swiglu(modulate(rmsnorm(h, norm_ffn2[layer]), mod[6], mod[7]), ffn2_w13[layer], ffn2_w2[layer])
        if ctx_out:
            hc = hc + bcast(mod_c[5]) * mix_ctx
            hc = hc + 0.5 * bcast(mod_c[8]) * swiglu(modulate(rmsnorm(hc, norm_ffn2[layer]), mod_c[6], mod_c[7]), ffn2_w13[layer], ffn2_w2[layer])
    return rmsnorm(h, final_norm)
```

```python
import functools
import math

import numpy as np
import jax
import jax.numpy as jnp
from jax import lax
from jax.experimental import pallas as pl
from jax.experimental.pallas import tpu as pltpu

F32 = jnp.float32
BF16 = jnp.bfloat16

GRID_W = 64
MLA_HEADS = 8
MLA_NOPE = 64
MLA_ROPE = 32
MLA_V = 64
MLA_Q_RANK = 384
MLA_KV_RANK = 256
SSM_WIDTH = 512
SSM_GROUP = 16
SSM_GROUPS = SSM_WIDTH // SSM_GROUP
SSM_STATE = 64
GQA_HEADS = 8
GQA_KV_HEADS = 2
GQA_HEAD_DIM = 64
WINDOW = 128
N_BRANCH = 3
N_MOD = 9
ROPE_BASE = 10000.0
EPS = 1e-6
NEG_INF = -1e30
LOG2E = math.log2(math.e)

LANES = 128
SUBLANES = 8
VMEM_LIMIT = 56 * 1024 * 1024
CHUNK = 16
PAIR = 2
HEAD_BLK = LANES
GQA_ORDER = (0, 4, 1, 5, 2, 6, 3, 7)


def _dot(a, b):
    return jnp.dot(a, b, preferred_element_type=F32)


def _dot_nt(a, b):
    return lax.dot_general(a, b, (((1,), (1,)), ((), ())), preferred_element_type=F32)


def _rms(x, g):
    return x * lax.rsqrt(jnp.mean(x * x, axis=-1, keepdims=True) + EPS) * g


def _sigmoid(x):
    return 1.0 / (1.0 + jnp.exp(-x))


def _params(sem, vmem=VMEM_LIMIT):
    return pltpu.CompilerParams(dimension_semantics=sem, vmem_limit_bytes=vmem)


def _resident(shape):
    nd = len(shape)
    return pl.BlockSpec(shape, lambda *_: (0,) * nd, pipeline_mode=pl.Buffered(1))


def _ada_kernel(c_ref, w_ref, b_ref, o_ref):
    c = c_ref[...]
    s = c * _sigmoid(c)
    w = w_ref[...]
    s_hi = s.astype(BF16)
    s_lo = (s - s_hi.astype(F32)).astype(BF16)
    w_hi = w.astype(BF16)
    w_lo = (w - w_hi.astype(F32)).astype(BF16)
    o_ref[...] = _dot(s_hi, w_hi) + _dot(s_hi, w_lo) + _dot(s_lo, w_hi) + b_ref[...]


def _ada(cc, ada_w, ada_b):
    L, D, N = ada_w.shape
    tn = 1152 if N % 1152 == 0 else N
    R = cc.shape[0]
    return pl.pallas_call(
        _ada_kernel,
        grid=(L, N // tn),
        in_specs=[pl.BlockSpec((R, D), lambda l, j: (0, 0)),
                  pl.BlockSpec((None, D, tn), lambda l, j: (l, 0, j)),
                  pl.BlockSpec((None, 1, tn), lambda l, j: (l, 0, j))],
        out_specs=pl.BlockSpec((None, R, tn), lambda l, j: (l, 0, j)),
        out_shape=jax.ShapeDtypeStruct((L, R, N), F32),
        compiler_params=_params(("arbitrary", "arbitrary")),
        name="ada",
    )(cc, ada_w, ada_b.reshape(L, 1, N))


def _ffn_kernel(h_ref, mod_ref, g_ref, w13_ref, w2_ref, *rest, k0, ff, tf, final):
    if final:
        fg_ref, o_ref = rest
    else:
        (o_ref,) = rest
    x = h_ref[...]
    mod = mod_ref[...]
    xn = _rms(x, g_ref[...]) * (1.0 + mod[k0 + 1:k0 + 2]) + mod[k0:k0 + 1]
    xb = xn.astype(BF16)
    acc = None
    for f0 in range(0, ff, tf):
        a = _dot(xb, w13_ref[:, f0:f0 + tf])
        b = _dot(xb, w13_ref[:, ff + f0:ff + f0 + tf])
        act = (a * _sigmoid(a) * b).astype(BF16)
        part = _dot(act, w2_ref[f0:f0 + tf, :])
        acc = part if acc is None else acc + part
    out = x + (0.5 * mod[k0 + 2:k0 + 3]) * acc
    if final:
        out = _rms(out, fg_ref[...])
    o_ref[...] = out


def _ffn(h, mod, g, w13, w2, k0, final_g=None):
    B, R, D = h.shape
    ff = w2.shape[0]
    tm = min(512, R)
    tf = ff // 2 if (ff // 2) % LANES == 0 else ff
    final = final_g is not None
    in_specs = [pl.BlockSpec((None, tm, D), lambda b, i: (b, i, 0)),
                pl.BlockSpec((None, N_MOD, D), lambda b, i: (b, 0, 0)),
                _resident((1, D)),
                _resident(w13.shape),
                _resident(w2.shape)]
    args = [h, mod, g.reshape(1, D), w13, w2]
    if final:
        in_specs.append(_resident((1, D)))
        args.append(final_g.reshape(1, D))
    return pl.pallas_call(
        functools.partial(_ffn_kernel, k0=k0, ff=ff, tf=tf, final=final),
        grid=(B, R // tm),
        in_specs=in_specs,
        out_specs=pl.BlockSpec((None, tm, D), lambda b, i: (b, i, 0)),
        out_shape=jax.ShapeDtypeStruct((B, R, D), F32),
        compiler_params=_params(("arbitrary", "arbitrary")),
        name="ffn",
    )(*args)


_O_CQ = 0
_O_CKV = _O_CQ + MLA_Q_RANK
_O_U = _O_CKV + MLA_KV_RANK
_O_GQ = _O_U + SSM_WIDTH
_O_GQR = _O_GQ + GQA_HEADS * GQA_HEAD_DIM
_O_GK = _O_GQR + GQA_HEADS * GQA_HEAD_DIM
_O_GKR = _O_GK + GQA_KV_HEADS * GQA_HEAD_DIM
_O_GV = _O_GKR + GQA_KV_HEADS * GQA_HEAD_DIM
_O_GATE = _O_GV + GQA_KV_HEADS * GQA_HEAD_DIM
_O_KR = _O_GATE + N_BRANCH * 1024
_W1_COLS = _O_KR + LANES
_QW = MLA_HEADS * HEAD_BLK
_VW = MLA_HEADS * MLA_V
_GQW = GQA_HEADS * GQA_HEAD_DIM
_GKW = GQA_KV_HEADS * GQA_HEAD_DIM


def _inproj_kernel(h_ref, mod_ref, g_ref, w1_ref, qn_ref, kvn_ref, wq_ref, wkv_ref, e_ref, tabm_ref, tabg_ref,
                   q_ref, k_ref, v_ref, u_ref, gq_ref, gk_ref, gv_ref, gate_ref, *, d_model):
    x = h_ref[...]
    mod = mod_ref[...]
    xb = (_rms(x, g_ref[...]) * (1.0 + mod[4:5]) + mod[3:4]).astype(BF16)

    def proj(o, n):
        return _dot(xb, w1_ref[:, o:o + n])

    cqn = _rms(proj(_O_CQ, MLA_Q_RANK), qn_ref[...]).astype(BF16)
    qq = _dot(cqn, wq_ref[...])
    cos_q = tabm_ref[:, 0:LANES]
    sin_q = tabm_ref[:, LANES:2 * LANES]
    for h in range(MLA_HEADS):
        a = qq[:, h * HEAD_BLK:(h + 1) * HEAD_BLK]
        b = qq[:, _QW + h * HEAD_BLK:_QW + (h + 1) * HEAD_BLK]
        q_ref[:, h * HEAD_BLK:(h + 1) * HEAD_BLK] = (a * cos_q + b * sin_q).astype(BF16)
    ckvn = _rms(proj(_O_CKV, MLA_KV_RANK), kvn_ref[...]).astype(BF16)
    kr = (proj(_O_KR, LANES) * tabm_ref[:, 2 * LANES:3 * LANES]).astype(BF16)
    k_ref[...] = (_dot(ckvn, wkv_ref[:, 0:_QW]) + _dot(kr, e_ref[...])).astype(BF16)
    v_ref[...] = _dot(ckvn, wkv_ref[:, _QW:_QW + _VW]).astype(BF16)
    u_ref[...] = proj(_O_U, SSM_WIDTH).astype(BF16)
    cos_gq = tabg_ref[:, 0:LANES]
    sin_gq = tabg_ref[:, LANES:2 * LANES]
    cos_gk = tabg_ref[:, 2 * LANES:3 * LANES]
    sin_gk = tabg_ref[:, 3 * LANES:4 * LANES]
    gq = proj(_O_GQ, _GQW)
    gqr = proj(_O_GQR, _GQW)
    for j in range(_GQW // LANES):
        sl = slice(j * LANES, (j + 1) * LANES)
        gq_ref[:, sl] = (gq[:, sl] * cos_gq + gqr[:, sl] * sin_gq).astype(BF16)
    gk_ref[...] = (proj(_O_GK, _GKW) * cos_gk + proj(_O_GKR, _GKW) * sin_gk).astype(BF16)
    gv_ref[...] = proj(_O_GV, _GKW).astype(BF16)
    gate_ref[...] = _sigmoid(proj(_O_GATE, N_BRANCH * d_model)).astype(BF16)


def _inproj(h, mod, g, lw, tabm, tabg):
    B, R, D = h.shape
    tm = min(512, R)
    widths = (_QW, _QW, _VW, SSM_WIDTH, _GQW, _GKW, _GKW, N_BRANCH * D)
    row = lambda w: pl.BlockSpec((None, tm, w), lambda b, i: (b, i, 0))
    return pl.pallas_call(
        functools.partial(_inproj_kernel, d_model=D),
        grid=(B, R // tm),
        in_specs=[row(D),
                  pl.BlockSpec((None, N_MOD, D), lambda b, i: (b, 0, 0)),
                  _resident((1, D)),
                  _resident(lw["w1"].shape),
                  _resident((1, MLA_Q_RANK)),
                  _resident((1, MLA_KV_RANK)),
                  _resident(lw["wq"].shape),
                  _resident(lw["wkv"].shape),
                  _resident(lw["e"].shape),
                  pl.BlockSpec((tm, 4 * LANES), lambda b, i: (i, 0)),
                  pl.BlockSpec((tm, 4 * LANES), lambda b, i: (i, 0))],
        out_specs=[row(w) for w in widths],
        out_shape=[jax.ShapeDtypeStruct((B, R, w), BF16) for w in widths],
        compiler_params=_params(("arbitrary", "arbitrary")),
        name="inproj",
    )(h, mod, g.reshape(1, D), lw["w1"], lw["qn"], lw["kvn"], lw["wq"], lw["wkv"], lw["e"], tabm, tabg)


def _mla_kernel(q_ref, kc_ref, vc_ref, *rest, tk, n_chunks):
    if n_chunks:
        k_ref, v_ref, o_ref = rest
    else:
        (o_ref,) = rest
    tq = q_ref.shape[0]
    outs = []
    for hh in range(2):
        hs = slice(hh * HEAD_BLK, (hh + 1) * HEAD_BLK)
        q = q_ref[:, hs]

        def step(kb, vb, m, l, acc):
            s = _dot_nt(q, kb)
            m_new = jnp.maximum(m, jnp.max(s, axis=-1, keepdims=True))
            alpha = jnp.exp2(m - m_new)
            p = jnp.exp2(s - m_new)
            l = alpha * l + jnp.sum(p, axis=-1, keepdims=True)
            acc = alpha * acc + _dot(p.astype(BF16), vb)
            return m_new, l, acc

        carry = step(kc_ref[:, hs], vc_ref[...],
                     jnp.full((tq, 1), NEG_INF, F32), jnp.zeros((tq, 1), F32), jnp.zeros((tq, LANES), F32))
        if n_chunks:
            def body(j, c):
                r = pl.ds(pl.multiple_of(j * tk, tk), tk)
                return step(k_ref[r, hs], v_ref[r, :], *c)
            carry = lax.fori_loop(0, n_chunks, body, carry)
        _, l, acc = carry
        outs.append(acc / l)
    lane = lax.broadcasted_iota(jnp.int32, (tq, LANES), 1)
    o_ref[...] = jnp.where(lane < MLA_V, outs[0], outs[1]).astype(BF16)


def _mla(q, kc, vc, k=None, v=None):
    B, Tq, _ = q.shape
    C = kc.shape[1]
    tq = min(512, Tq)
    nh2 = MLA_HEADS // 2
    in_specs = [pl.BlockSpec((None, tq, 2 * HEAD_BLK), lambda b, h, i: (b, i, h)),
                pl.BlockSpec((None, C, 2 * HEAD_BLK), lambda b, h, i: (b, 0, h)),
                pl.BlockSpec((None, C, 2 * MLA_V), lambda b, h, i: (b, 0, h))]
    args = [q, kc, vc]
    n_chunks, tk = 0, 0
    if k is not None:
        T = k.shape[1]
        tk = min(512, T)
        n_chunks = T // tk
        in_specs += [pl.BlockSpec((None, T, 2 * HEAD_BLK), lambda b, h, i: (b, 0, h)),
                     pl.BlockSpec((None, T, 2 * MLA_V), lambda b, h, i: (b, 0, h))]
        args += [k, v]
    return pl.pallas_call(
        functools.partial(_mla_kernel, tk=tk, n_chunks=n_chunks),
        grid=(B, nh2, Tq // tq),
        in_specs=in_specs,
        out_specs=pl.BlockSpec((None, tq, 2 * MLA_V), lambda b, h, i: (b, i, h)),
        out_shape=jax.ShapeDtypeStruct((B, Tq, _VW), BF16),
        compiler_params=_params(("arbitrary", "arbitrary", "arbitrary")),
        name="mla",
    )(*args)


def _gqa_kernel(sink_ref, q_ref, kc_ref, vc_ref, *rest, tq, band, seq):
    if band:
        k_ref, v_ref, o_ref = rest
    else:
        (o_ref,) = rest
    half = GQA_HEAD_DIM
    lane = lax.broadcasted_iota(jnp.int32, (tq, LANES), 1)
    kc = kc_ref[...]
    vc = vc_ref[...]
    if band:
        nk = tq + 2 * WINDOW
        q0 = pl.program_id(1) * tq
        start = pl.multiple_of(jnp.clip(q0 - WINDOW, 0, seq - nk), LANES)
        kw = k_ref[pl.ds(start, nk), :]
        vw = v_ref[pl.ds(start, nk), :]
        r = lax.broadcasted_iota(jnp.int32, (2 * tq, nk), 0)
        qpos = q0 + jnp.where(r < tq, r, r - tq)
        kpos = start + lax.broadcasted_iota(jnp.int32, (2 * tq, nk), 1)
        valid = jnp.abs(qpos - kpos) <= WINDOW
    rows = lax.broadcasted_iota(jnp.int32, (2 * tq, 1), 0)
    for j in range(GQA_HEADS // 2):
        qb = q_ref[:, j * LANES:(j + 1) * LANES]
        zero = jnp.zeros_like(qb)
        qq = jnp.concatenate([jnp.where(lane < half, qb, zero), jnp.where(lane >= half, qb, zero)], axis=0)
        sink = jnp.where(rows < tq, sink_ref[j], sink_ref[GQA_HEADS // 2 + j])
        s_c = _dot_nt(qq, kc)
        m = jnp.maximum(sink, jnp.max(s_c, axis=-1, keepdims=True))
        if band:
            s_b = jnp.where(valid, _dot_nt(qq, kw), NEG_INF)
            m = jnp.maximum(m, jnp.max(s_b, axis=-1, keepdims=True))
        p_c = jnp.exp2(s_c - m)
        den = jnp.exp2(sink - m) + jnp.sum(p_c, axis=-1, keepdims=True)
        o = _dot(p_c.astype(BF16), vc)
        if band:
            p_b = jnp.exp2(s_b - m)
            den = den + jnp.sum(p_b, axis=-1, keepdims=True)
            o = o + _dot(p_b.astype(BF16), vw)
        o = o / den
        o_ref[:, j * LANES:(j + 1) * LANES] = jnp.where(lane < half, o[:tq], o[tq:]).astype(BF16)


def _gqa(sink, q, kc, vc, k=None, v=None):
    B, Tq, _ = q.shape
    C = kc.shape[1]
    band = k is not None
    tq = min(256, Tq)
    full = lambda n: pl.BlockSpec((None, n, _GKW), lambda b, i: (b, 0, 0))
    in_specs = [pl.BlockSpec(memory_space=pltpu.SMEM),
                pl.BlockSpec((None, tq, _GQW), lambda b, i: (b, i, 0)),
                full(C), full(C)]
    args = [sink, q, kc, vc]
    seq = 0
    if band:
        seq = k.shape[1]
        assert seq >= tq + 2 * WINDOW
        in_specs += [full(seq), full(seq)]
        args += [k, v]
    return pl.pallas_call(
        functools.partial(_gqa_kernel, tq=tq, band=band, seq=seq),
        grid=(B, Tq // tq),
        in_specs=in_specs,
        out_specs=pl.BlockSpec((None, tq, _GQW), lambda b, i: (b, i, 0)),
        out_shape=jax.ShapeDtypeStruct((B, Tq, _GQW), BF16),
        compiler_params=_params(("arbitrary", "arbitrary")),
        name="gqa",
    )(*args)


def _gelu(y):
    return 0.5 * y * (1.0 + jnp.tanh(math.sqrt(2.0 / math.pi) * (y + 0.044715 * (y * y * y))))


def _ssm_kernel(u_ref, we_ref, wy_ref, a_ref, y_ref, e_ref, sf_ref, sr_ref, *, n_ctx, n_all, nsplit):
    sw = PAIR * SSM_STATE
    rows = n_all * SUBLANES
    rb = rows // nsplit
    for s in range(nsplit):
        r = slice(s * rb, (s + 1) * rb)
        e_ref[r, :] = _dot(u_ref[r, :], we_ref[...])
    a = a_ref[...]
    af_re, af_im = a[0:1, 0:sw], a[0:1, sw:2 * sw]
    ar_re, ar_im = a[1:2, 0:sw], a[1:2, sw:2 * sw]
    zero = jnp.zeros((SUBLANES, sw), F32)

    def rows_of(c):
        return pl.ds(pl.multiple_of(c * SUBLANES, SUBLANES), SUBLANES)

    def fwd(c, carry):
        s_re, s_im = carry
        r = rows_of(c)
        sf_ref[r, 0:sw] = s_re
        sf_ref[r, sw:2 * sw] = s_im
        return (af_re * s_re - af_im * s_im + e_ref[r, 0:sw],
                af_re * s_im + af_im * s_re + e_ref[r, sw:2 * sw])

    lax.fori_loop(0, n_all, fwd, (zero, zero))

    def rev_at(c, carry):
        s_re, s_im = carry
        r = rows_of(c)
        sr_ref[r, 0:sw] = s_re
        sr_ref[r, sw:2 * sw] = s_im
        return (ar_re * s_re - ar_im * s_im + e_ref[r, 2 * sw:3 * sw],
                ar_re * s_im + ar_im * s_re + e_ref[r, 3 * sw:4 * sw])

    carry = lax.fori_loop(0, n_ctx, lambda i, c: rev_at(n_ctx - 1 - i, c), (zero, zero))
    lax.fori_loop(0, n_all - n_ctx, lambda i, c: rev_at(n_all - 1 - i, c), carry)

    uw = u_ref.shape[1]
    for s in range(nsplit):
        r = slice(s * rb, (s + 1) * rb)
        y = (_dot(u_ref[r, :], wy_ref[0:uw, :])
             + _dot(sf_ref[r, :].astype(BF16), wy_ref[uw:uw + 2 * sw, :])
             + _dot(sr_ref[r, :].astype(BF16), wy_ref[uw + 2 * sw:uw + 4 * sw, :]))
        y_ref[r, :] = _gelu(y).astype(BF16)


def _ssm(u_lat, u_ctx, we, wy, a):
    B, T, W = u_lat.shape
    C = u_ctx.shape[1]
    n_ctx, n_all = C // CHUNK, (C + T) // CHUNK
    npair = SSM_GROUPS // PAIR
    uw = PAIR * CHUNK * SSM_GROUP
    u = jnp.concatenate([u_ctx, u_lat], axis=1).reshape(B, n_all, CHUNK, npair, PAIR, SSM_GROUP)
    u = jnp.transpose(u, (3, 1, 0, 4, 2, 5))
    u = jnp.pad(u, ((0, 0), (0, 0), (0, SUBLANES - B), (0, 0), (0, 0), (0, 0))).reshape(npair, n_all * SUBLANES, uw)
    rows = n_all * SUBLANES
    nsplit = 4 if n_all % 4 == 0 else 1
    sw2 = 2 * PAIR * SSM_STATE
    y = pl.pallas_call(
        functools.partial(_ssm_kernel, n_ctx=n_ctx, n_all=n_all, nsplit=nsplit),
        grid=(npair,),
        in_specs=[pl.BlockSpec((None, rows, uw), lambda g: (g, 0, 0)),
                  pl.BlockSpec((None, uw, 2 * sw2), lambda g: (g, 0, 0)),
                  pl.BlockSpec((None, uw + 2 * sw2, uw), lambda g: (g, 0, 0)),
                  pl.BlockSpec((None, SUBLANES, sw2), lambda g: (g, 0, 0))],
        out_specs=pl.BlockSpec((None, rows, uw), lambda g: (g, 0, 0)),
        out_shape=jax.ShapeDtypeStruct((npair, rows, uw), BF16),
        scratch_shapes=[pltpu.VMEM((rows, 2 * sw2), F32),
                        pltpu.VMEM((rows, sw2), F32),
                        pltpu.VMEM((rows, sw2), F32)],
        compiler_params=_params(("arbitrary",)),
        name="ssm",
    )(u, we, wy, a)
    y = y.reshape(npair, n_all, SUBLANES, PAIR, CHUNK, SSM_GROUP)[:, :, :B]
    y = jnp.transpose(y, (2, 1, 4, 0, 3, 5)).reshape(B, C + T, W)
    return y[:, C:], y[:, :C]


def _merge_kernel(h_ref, mod_ref, a_ref, s_ref, g_ref, gate_ref, wo_ref, wglu_ref, wgo_ref, wout_ref, o_ref,
                  *, d_model):
    D = d_model
    b0 = _dot(a_ref[...], wo_ref[...])
    t = _dot(s_ref[...], wglu_ref[...])
    b1 = t[:, :D] * _sigmoid(t[:, D:])
    b2 = _dot(g_ref[...], wgo_ref[...])
    gate = gate_ref[...].astype(F32)
    mix = gate[:, 0:D] * b0 + gate[:, D:2 * D] * b1 + gate[:, 2 * D:3 * D] * b2
    o_ref[...] = h_ref[...] + mod_ref[5:6, :] * _dot(mix.astype(BF16), wout_ref[...])


def _merge(h, mod, attn, yssm, gqa, gate, lw):
    B, R, D = h.shape
    tm = min(512, R)
    row = lambda w: pl.BlockSpec((None, tm, w), lambda b, i: (b, i, 0))
    return pl.pallas_call(
        functools.partial(_merge_kernel, d_model=D),
        grid=(B, R // tm),
        in_specs=[row(D),
                  pl.BlockSpec((None, N_MOD, D), lambda b, i: (b, 0, 0)),
                  row(_VW), row(SSM_WIDTH), row(_GQW), row(N_BRANCH * D),
                  _resident(lw["wo"].shape), _resident(lw["wglu"].shape),
                  _resident(lw["wgo"].shape), _resident(lw["wout"].shape)],
        out_specs=row(D),
        out_shape=jax.ShapeDtypeStruct((B, R, D), F32),
        compiler_params=_params(("arbitrary", "arbitrary")),
        name="merge",
    )(h, mod, attn, yssm, gqa, gate, lw["wo"], lw["wglu"], lw["wgo"], lw["wout"])


def _rot_index(width, n_axial):
    h = n_axial // 2
    src = np.arange(width)
    sign = np.ones(width, np.float32)
    for base in range(0, width, n_axial):
        for i in range(h):
            src[base + i], sign[base + i] = base + i + h, -1.0
            src[base + i + h], sign[base + i + h] = base + i, 1.0
    return src, sign


def _rot_cols(w, n_axial):
    src, sign = _rot_index(w.shape[-1], n_axial)
    return w[..., src] * sign


def _rope_angles(T, n):
    t = np.arange(T)
    inv = ROPE_BASE ** (-np.arange(0, n, 2, dtype=np.float32) / n)
    out = []
    for pos in (t // GRID_W, t % GRID_W):
        ang = jnp.asarray(pos.astype(np.float32))[:, None] * jnp.asarray(inv)[None, :]
        out.append((jnp.cos(ang), jnp.sin(ang)))
    cos = jnp.concatenate([out[0][0], out[0][0], out[1][0], out[1][0]], axis=1)
    sin = jnp.concatenate([out[0][1], out[0][1], out[1][1], out[1][1]], axis=1)
    return cos, sin


def _tables(T, C):
    sc_m = (MLA_NOPE + MLA_ROPE) ** -0.5 * LOG2E
    sc_g = GQA_HEAD_DIM ** -0.5 * LOG2E
    cm, sm = _rope_angles(T, MLA_ROPE // 2)
    cg, sg = _rope_angles(T, GQA_HEAD_DIM // 2)
    one = lambda n, w: jnp.ones((n, w), F32)
    zero = lambda n, w: jnp.zeros((n, w), F32)
    pad = LANES - MLA_NOPE - MLA_ROPE

    def mla_tab(n, c, s):
        cos_q = jnp.concatenate([one(n, MLA_NOPE), c, zero(n, pad)], axis=1) * sc_m
        sin_q = jnp.concatenate([zero(n, MLA_NOPE), s, zero(n, pad)], axis=1) * sc_m
        kr = jnp.concatenate([c, s, zero(n, LANES - 2 * MLA_ROPE)], axis=1)
        return jnp.concatenate([cos_q, sin_q, kr, zero(n, LANES)], axis=1)

    def gqa_tab(n, c, s):
        c2, s2 = jnp.concatenate([c, c], axis=1), jnp.concatenate([s, s], axis=1)
        return jnp.concatenate([c2 * sc_g, s2 * sc_g, c2, s2], axis=1)

    lat = (mla_tab(T, cm, sm), gqa_tab(T, cg, sg))
    ctx = (mla_tab(C, one(C, MLA_ROPE), zero(C, MLA_ROPE)),
           gqa_tab(C, one(C, GQA_HEAD_DIM), zero(C, GQA_HEAD_DIM)))
    return lat, ctx


def _place_matrix():
    e = np.zeros((LANES, _QW), np.float32)
    for h in range(MLA_HEADS):
        for j in range(MLA_ROPE):
            e[j, h * HEAD_BLK + MLA_NOPE + j] = 1.0
            e[MLA_ROPE + j, h * HEAD_BLK + MLA_NOPE + j] = 1.0
    return jnp.asarray(e, BF16)


def _layer_weights(w_in, q_norm, kv_norm, w_uq, w_ukv, w_o, w_glu, gqa_w_o, w_out):
    D = w_in.shape[0]
    offs = np.cumsum([MLA_Q_RANK, MLA_KV_RANK, MLA_ROPE, SSM_WIDTH, _GQW, _GKW, _GKW]).tolist()
    cq, ckv, kr, u, gq, gk, gv, gates = jnp.split(w_in, offs, axis=1)
    order = np.asarray(GQA_ORDER)
    gq_h = gq.reshape(D, GQA_HEADS, GQA_HEAD_DIM)
    gq_r = _rot_cols(gq_h, GQA_HEAD_DIM // 2)
    gk_r = _rot_cols(gk.reshape(D, GQA_KV_HEADS, GQA_HEAD_DIM), GQA_HEAD_DIM // 2).reshape(D, _GKW)
    kr_blk = jnp.concatenate([kr, _rot_cols(kr, MLA_ROPE // 2), jnp.zeros((D, LANES - 2 * MLA_ROPE), F32)], axis=1)
    w1 = jnp.concatenate([cq, ckv, u, gq_h[:, order].reshape(D, _GQW), gq_r[:, order].reshape(D, _GQW),
                          gk, gk_r, gv, gates, kr_blk], axis=1).astype(BF16)
    uq = w_uq.reshape(MLA_Q_RANK, MLA_HEADS, MLA_NOPE + MLA_ROPE)
    nope, rope = uq[..., :MLA_NOPE], uq[..., MLA_NOPE:]
    zpad = jnp.zeros((MLA_Q_RANK, MLA_HEADS, HEAD_BLK - MLA_NOPE - MLA_ROPE), F32)
    wq_a = jnp.concatenate([nope, rope, zpad], axis=-1).reshape(MLA_Q_RANK, _QW)
    wq_b = jnp.concatenate([jnp.zeros_like(nope), _rot_cols(rope, MLA_ROPE // 2), zpad], axis=-1)
    wq = jnp.concatenate([wq_a, wq_b.reshape(MLA_Q_RANK, _QW)], axis=1).astype(BF16)
    ukv = w_ukv.reshape(MLA_KV_RANK, MLA_HEADS, MLA_NOPE + MLA_V)
    wk = jnp.concatenate([ukv[..., :MLA_NOPE], jnp.zeros((MLA_KV_RANK, MLA_HEADS, HEAD_BLK - MLA_NOPE), F32)],
                         axis=-1).reshape(MLA_KV_RANK, _QW)
    wv = ukv[..., MLA_NOPE:].reshape(MLA_KV_RANK, _VW)
    wkv = jnp.concatenate([wk, wv], axis=1).astype(BF16)
    wgo = gqa_w_o.reshape(GQA_HEADS, GQA_HEAD_DIM, D)[order].reshape(_GQW, D).astype(BF16)
    return dict(w1=w1, qn=q_norm.reshape(1, -1), kvn=kv_norm.reshape(1, -1), wq=wq, wkv=wkv, e=_place_matrix(),
                wo=w_o.astype(BF16), wglu=w_glu.astype(BF16), wgo=wgo, wout=w_out.astype(BF16))


def _ssm_weights(lam_re, lam_im, log_dt, b_re, b_im, c_re, c_im, d_skip):
    hp = lax.Precision.HIGHEST
    G, P, M, Lc = SSM_GROUPS, SSM_STATE, SSM_GROUP, CHUNK
    dt = jnp.exp(log_dt)[..., None]
    kk = jnp.arange(Lc + 1, dtype=F32)[:, None, None, None]
    mag = jnp.exp(lam_re[None] * dt[None] * kk)
    pw_re, pw_im = mag * jnp.cos(lam_im[None] * dt[None] * kk), mag * jnp.sin(lam_im[None] * dt[None] * kk)
    a_re, a_im = pw_re[1], pw_im[1]
    den = lam_re * lam_re + lam_im * lam_im
    w_re = ((a_re - 1) * lam_re + a_im * lam_im) / den
    w_im = (a_im * lam_re - (a_re - 1) * lam_im) / den
    bb_re = w_re[..., None] * b_re - w_im[..., None] * b_im
    bb_im = w_re[..., None] * b_im + w_im[..., None] * b_re
    ca_re = c_re[:, None] * jnp.moveaxis(pw_re, 0, 1)[:, :, :, None, :] - c_im[:, None] * jnp.moveaxis(pw_im, 0, 1)[:, :, :, None, :]
    ca_im = c_re[:, None] * jnp.moveaxis(pw_im, 0, 1)[:, :, :, None, :] + c_im[:, None] * jnp.moveaxis(pw_re, 0, 1)[:, :, :, None, :]
    cb = (jnp.einsum('dkgop,dgpm->dkgom', ca_re, bb_re, precision=hp)
          - jnp.einsum('dkgop,dgpm->dkgom', ca_im, bb_im, precision=hp))
    lag = np.arange(Lc)[None, :] - np.arange(Lc)[:, None]
    kf = jnp.where((lag >= 0)[None, :, :, None, None], cb[0][np.clip(lag, 0, Lc)].transpose(2, 0, 1, 3, 4), 0.0)
    kr = jnp.where((lag <= 0)[None, :, :, None, None], cb[1][np.clip(-lag, 0, Lc)].transpose(2, 0, 1, 3, 4), 0.0)
    kmat = kf + kr
    skip = d_skip.reshape(G, M)
    eye = jnp.eye(Lc, dtype=F32)[None, :, :, None, None] * (jnp.eye(M, dtype=F32)[None, None, None] * skip[:, None, None, :, None])
    kmat = jnp.transpose(kmat + eye, (0, 1, 4, 2, 3)).reshape(G, Lc * M, Lc * M)
    idx_f = np.arange(Lc)[::-1].copy()
    idx_r = np.arange(Lc)

    def drive(d, idx):
        p_re, p_im = pw_re[idx, d], pw_im[idx, d]
        e_re = p_re[..., None] * bb_re[d][None] - p_im[..., None] * bb_im[d][None]
        e_im = p_re[..., None] * bb_im[d][None] + p_im[..., None] * bb_re[d][None]
        to = lambda t: jnp.transpose(t, (1, 0, 3, 2)).reshape(G, Lc * M, P)
        return to(e_re), to(e_im)

    ef_re, ef_im = drive(0, idx_f)
    er_re, er_im = drive(1, idx_r)

    def readout(d, idx):
        q_re, q_im = ca_re[d][idx], ca_im[d][idx]
        to = lambda t: jnp.transpose(t, (1, 3, 0, 2)).reshape(G, P, Lc * M)
        return to(q_re), to(-q_im)

    qf_re, qf_im = readout(0, np.arange(1, Lc + 1))
    qr_re, qr_im = readout(1, Lc - np.arange(Lc))
    npair = G // PAIR

    def pair_diag(t):
        R, Cn = t.shape[1:]
        t = t.reshape(npair, PAIR, R, Cn)
        out = jnp.zeros((npair, PAIR, R, PAIR, Cn), F32)
        for p in range(PAIR):
            out = out.at[:, p, :, p, :].set(t[:, p])
        return out.reshape(npair, PAIR * R, PAIR * Cn)

    we = jnp.concatenate([pair_diag(ef_re), pair_diag(ef_im), pair_diag(er_re), pair_diag(er_im)], axis=2)
    wy = jnp.concatenate([pair_diag(kmat), pair_diag(qf_re), pair_diag(qf_im), pair_diag(qr_re), pair_diag(qr_im)], axis=1)
    sw = PAIR * P
    a_step = jnp.stack([jnp.concatenate([pw_re[Lc, 0].reshape(npair, sw), pw_im[Lc, 0].reshape(npair, sw)], axis=1),
                        jnp.concatenate([pw_re[Lc, 1].reshape(npair, sw), pw_im[Lc, 1].reshape(npair, sw)], axis=1)], axis=1)
    a_step = jnp.pad(a_step, ((0, 0), (0, SUBLANES - 2), (0, 0)))
    return we.astype(BF16), wy.astype(BF16), a_step


def kernel(x, c, ctx, c_ctx, ada_w, ada_b, norm_ffn1, norm_mix, norm_ffn2, ffn1_w13, ffn1_w2, ffn2_w13, ffn2_w2, w_in, mla_q_norm, mla_kv_norm, mla_w_uq, mla_w_ukv, mla_w_o, ssm_lambda_re, ssm_lambda_im, ssm_log_dt, ssm_b_re, ssm_b_im, ssm_c_re, ssm_c_im, ssm_d, ssm_w_glu, gqa_sink, gqa_w_o, w_out, final_norm):
    B, T, D = x.shape
    C = ctx.shape[1]
    depth = ada_w.shape[0]
    cc = jnp.concatenate([c, c_ctx[None], jnp.zeros((SUBLANES - B - 1, D), F32)], axis=0)
    mods = _ada(cc, ada_w, ada_b).reshape(depth, SUBLANES, N_MOD, D)
    (tabm, tabg), (tabm_c, tabg_c) = _tables(T, C)
    sinks = gqa_sink * LOG2E
    h, hc = x, ctx
    for l in range(depth):
        ctx_out = l < depth - 1
        last = l == depth - 1
        mod = mods[l, :B]
        mod_c = jnp.broadcast_to(mods[l, B][None], (B, N_MOD, D))
        lw = _layer_weights(w_in[l], mla_q_norm[l], mla_kv_norm[l], mla_w_uq[l], mla_w_ukv[l], mla_w_o[l],
                            ssm_w_glu[l], gqa_w_o[l], w_out[l])
        we, wy, a_step = _ssm_weights(ssm_lambda_re[l], ssm_lambda_im[l], ssm_log_dt[l], ssm_b_re[l], ssm_b_im[l],
                                      ssm_c_re[l], ssm_c_im[l], ssm_d[l])
        w13a, w2a = ffn1_w13[l].astype(BF16), ffn1_w2[l].astype(BF16)
        h = _ffn(h, mod, norm_ffn1[l], w13a, w2a, 0)
        hc = _ffn(hc, mod_c, norm_ffn1[l], w13a, w2a, 0)
        q, k, v, u, gq, gk, gv, gate = _inproj(h, mod, norm_mix[l], lw, tabm, tabg)
        q_c, k_c, v_c, u_c, gq_c, gk_c, gv_c, gate_c = _inproj(hc, mod_c, norm_mix[l], lw, tabm_c, tabg_c)
        attn = _mla(q, k_c, v_c, k, v)
        y_lat, y_ctx = _ssm(u, u_c, we, wy, a_step)
        gqa = _gqa(sinks[l], gq, gk_c, gv_c, gk, gv)
        h = _merge(h, mod, attn, y_lat, gqa, gate, lw)
        w13b, w2b = ffn2_w13[l].astype(BF16), ffn2_w2[l].astype(BF16)
        h = _ffn(h, mod, norm_ffn2[l], w13b, w2b, 6, final_norm if last else None)
        if ctx_out:
            attn_c = _mla(q_c, k_c, v_c)
            gqa_c = _gqa(sinks[l], gq_c, gk_c, gv_c)
            hc = _merge(hc, mod_c, attn_c, y_ctx, gqa_c, gate_c, lw)
            hc = _ffn(hc, mod_c, norm_ffn2[l], w13b, w2b, 6)
    return h
```

```python
import functools
import math

import numpy as np
import jax
import jax.numpy as jnp
from jax import lax
from jax.experimental import pallas as pl
from jax.experimental.pallas import tpu as pltpu

F32 = jnp.float32
BF16 = jnp.bfloat16

GRID_W = 64
MLA_HEADS = 8
MLA_NOPE = 64
MLA_ROPE = 32
MLA_V = 64
MLA_Q_RANK = 384
MLA_KV_RANK = 256
SSM_WIDTH = 512
SSM_GROUP = 16
SSM_GROUPS = SSM_WIDTH // SSM_GROUP
SSM_STATE = 64
GQA_HEADS = 8
GQA_KV_HEADS = 2
GQA_HEAD_DIM = 64
WINDOW = 128
N_BRANCH = 3
N_MOD = 9
ROPE_BASE = 10000.0
EPS = 1e-6
NEG_INF = -1e30
LOG2E = math.log2(math.e)

LANES = 128
SUBLANES = 8
VMEM_LIMIT = 56 * 1024 * 1024
CHUNK = 16
PAIR = 2
HEAD_BLK = LANES
MLA_TQ = 512
MLA_TK = 2048
GQA_ORDER = (0, 4, 1, 5, 2, 6, 3, 7)


def _dot(a, b):
    return jnp.dot(a, b, preferred_element_type=F32)


def _dot_nt(a, b):
    return lax.dot_general(a, b, (((1,), (1,)), ((), ())), preferred_element_type=F32)


def _rms(x, g):
    return x * lax.rsqrt(jnp.mean(x * x, axis=-1, keepdims=True) + EPS) * g


def _sigmoid(x):
    return 1.0 / (1.0 + jnp.exp(-x))


def _params(sem, vmem=VMEM_LIMIT):
    return pltpu.CompilerParams(dimension_semantics=sem, vmem_limit_bytes=vmem)


def _resident(shape):
    nd = len(shape)
    return pl.BlockSpec(shape, lambda *_: (0,) * nd, pipeline_mode=pl.Buffered(1))


def _ada_kernel(c_ref, w_ref, b_ref, o_ref):
    c = c_ref[...]
    s = c * _sigmoid(c)
    w = w_ref[...]
    s_hi = s.astype(BF16)
    s_lo = (s - s_hi.astype(F32)).astype(BF16)
    w_hi = w.astype(BF16)
    w_lo = (w - w_hi.astype(F32)).astype(BF16)
    o_ref[...] = _dot(s_hi, w_hi) + _dot(s_hi, w_lo) + _dot(s_lo, w_hi) + b_ref[...]


def _ada(cc, ada_w, ada_b):
    L, D, N = ada_w.shape
    tn = 1152 if N % 1152 == 0 else N
    R = cc.shape[0]
    return pl.pallas_call(
        _ada_kernel,
        grid=(L, N // tn),
        in_specs=[pl.BlockSpec((R, D), lambda l, j: (0, 0)),
                  pl.BlockSpec((None, D, tn), lambda l, j: (l, 0, j)),
                  pl.BlockSpec((None, 1, tn), lambda l, j: (l, 0, j))],
        out_specs=pl.BlockSpec((None, R, tn), lambda l, j: (l, 0, j)),
        out_shape=jax.ShapeDtypeStruct((L, R, N), F32),
        compiler_params=_params(("arbitrary", "arbitrary")),
        name="ada",
    )(cc, ada_w, ada_b.reshape(L, 1, N))


def _ffn_kernel(h_ref, mod_ref, g_ref, w13_ref, w2_ref, *rest, k0, ff, tf, final):
    if final:
        fg_ref, o_ref = rest
    else:
        (o_ref,) = rest
    x = h_ref[...]
    mod = mod_ref[...]
    xn = _rms(x, g_ref[...]) * (1.0 + mod[k0 + 1:k0 + 2]) + mod[k0:k0 + 1]
    xb = xn.astype(BF16)
    acc = None
    for f0 in range(0, ff, tf):
        a = _dot(xb, w13_ref[:, f0:f0 + tf])
        b = _dot(xb, w13_ref[:, ff + f0:ff + f0 + tf])
        act = (a * _sigmoid(a) * b).astype(BF16)
        part = _dot(act, w2_ref[f0:f0 + tf, :])
        acc = part if acc is None else acc + part
    out = x + (0.5 * mod[k0 + 2:k0 + 3]) * acc
    if final:
        out = _rms(out, fg_ref[...])
    o_ref[...] = out


def _ffn(h, mod, g, w13, w2, k0, final_g=None):
    B, R, D = h.shape
    ff = w2.shape[0]
    tm = min(512, R)
    tf = ff // 2 if (ff // 2) % LANES == 0 else ff
    final = final_g is not None
    in_specs = [pl.BlockSpec((None, tm, D), lambda b, i: (b, i, 0)),
                pl.BlockSpec((None, N_MOD, D), lambda b, i: (b, 0, 0)),
                _resident((1, D)),
                _resident(w13.shape),
                _resident(w2.shape)]
    args = [h, mod, g.reshape(1, D), w13, w2]
    if final:
        in_specs.append(_resident((1, D)))
        args.append(final_g.reshape(1, D))
    return pl.pallas_call(
        functools.partial(_ffn_kernel, k0=k0, ff=ff, tf=tf, final=final),
        grid=(B, R // tm),
        in_specs=in_specs,
        out_specs=pl.BlockSpec((None, tm, D), lambda b, i: (b, i, 0)),
        out_shape=jax.ShapeDtypeStruct((B, R, D), F32),
        compiler_params=_params(("arbitrary", "arbitrary")),
        name="ffn",
    )(*args)


_O_CQ = 0
_O_CKV = _O_CQ + MLA_Q_RANK
_O_U = _O_CKV + MLA_KV_RANK
_O_GQ = _O_U + SSM_WIDTH
_O_GQR = _O_GQ + GQA_HEADS * GQA_HEAD_DIM
_O_GK = _O_GQR + GQA_HEADS * GQA_HEAD_DIM
_O_GKR = _O_GK + GQA_KV_HEADS * GQA_HEAD_DIM
_O_GV = _O_GKR + GQA_KV_HEADS * GQA_HEAD_DIM
_O_GATE = _O_GV + GQA_KV_HEADS * GQA_HEAD_DIM
_O_KR = _O_GATE + N_BRANCH * 1024
_W1_COLS = _O_KR + LANES
_QW = MLA_HEADS * HEAD_BLK
_VW = MLA_HEADS * MLA_V
_GQW = GQA_HEADS * GQA_HEAD_DIM
_GKW = GQA_KV_HEADS * GQA_HEAD_DIM


_PIECES = LANES // SSM_GROUP
_ZW = CHUNK * SSM_GROUP


def _piece_masks(rows):
    lane = lax.broadcasted_iota(jnp.int32, (rows, LANES), 1)
    return [(lane >= i * SSM_GROUP) & (lane < (i + 1) * SSM_GROUP) for i in range(_PIECES)]


def _block_transpose(a, masks):
    out = []
    for d in range(_PIECES):
        acc = None
        for s in range(_PIECES):
            shift = ((s - d) % _PIECES) * SSM_GROUP
            r = pltpu.roll(a[s], shift, 1) if shift else a[s]
            acc = r if acc is None else jnp.where(masks[s], r, acc)
        out.append(acc)
    return out


def _inproj_kernel(h_ref, mod_ref, g_ref, w1_ref, qn_ref, kvn_ref, wq_ref, wkv_ref, e_ref, tabm_ref, tabg_ref,
                   q_ref, k_ref, v_ref, z_ref, gq_ref, gk_ref, gv_ref, gate_ref, us_ref, *, d_model):
    x = h_ref[...]
    mod = mod_ref[...]
    xb = (_rms(x, g_ref[...]) * (1.0 + mod[4:5]) + mod[3:4]).astype(BF16)

    def proj(o, n):
        return _dot(xb, w1_ref[:, o:o + n])

    cqn = _rms(proj(_O_CQ, MLA_Q_RANK), qn_ref[...]).astype(BF16)
    qq = _dot(cqn, wq_ref[...])
    cos_q = tabm_ref[:, 0:LANES]
    sin_q = tabm_ref[:, LANES:2 * LANES]
    for h in range(MLA_HEADS):
        a = qq[:, h * HEAD_BLK:(h + 1) * HEAD_BLK]
        b = qq[:, _QW + h * HEAD_BLK:_QW + (h + 1) * HEAD_BLK]
        q_ref[:, h * HEAD_BLK:(h + 1) * HEAD_BLK] = (a * cos_q + b * sin_q).astype(BF16)
    ckvn = _rms(proj(_O_CKV, MLA_KV_RANK), kvn_ref[...]).astype(BF16)
    kr = (proj(_O_KR, LANES) * tabm_ref[:, 2 * LANES:3 * LANES]).astype(BF16)
    k_ref[...] = (_dot(ckvn, wkv_ref[:, 0:_QW]) + _dot(kr, e_ref[...])).astype(BF16)
    v_ref[...] = _dot(ckvn, wkv_ref[:, _QW:_QW + _VW]).astype(BF16)
    u = proj(_O_U, SSM_WIDTH)
    nr = us_ref.shape[1] // CHUNK
    masks = _piece_masks(nr)
    for lt in range(SSM_WIDTH // LANES):
        us_ref[lt] = u[:, lt * LANES:(lt + 1) * LANES]
        for jh in range(_ZW // LANES):
            a = [us_ref[lt, pl.ds(jh * _PIECES + s, nr, stride=CHUNK), :] for s in range(_PIECES)]
            b = _block_transpose(a, masks)
            for d in range(_PIECES):
                z_ref[lt * _PIECES + d, :, jh * LANES:(jh + 1) * LANES] = b[d].astype(BF16)
    cos_gq = tabg_ref[:, 0:LANES]
    sin_gq = tabg_ref[:, LANES:2 * LANES]
    cos_gk = tabg_ref[:, 2 * LANES:3 * LANES]
    sin_gk = tabg_ref[:, 3 * LANES:4 * LANES]
    gq = proj(_O_GQ, _GQW)
    gqr = proj(_O_GQR, _GQW)
    for j in range(_GQW // LANES):
        sl = slice(j * LANES, (j + 1) * LANES)
        gq_ref[:, sl] = (gq[:, sl] * cos_gq + gqr[:, sl] * sin_gq).astype(BF16)
    gk_ref[...] = (proj(_O_GK, _GKW) * cos_gk + proj(_O_GKR, _GKW) * sin_gk).astype(BF16)
    gv_ref[...] = proj(_O_GV, _GKW).astype(BF16)
    gate_ref[...] = _sigmoid(proj(_O_GATE, N_BRANCH * d_model)).astype(BF16)


def _inproj(h, mod, g, lw, tabm, tabg):
    B, R, D = h.shape
    tm = min(512, R)
    widths = (_QW, _QW, _VW, None, _GQW, _GKW, _GKW, N_BRANCH * D)
    row = lambda w: pl.BlockSpec((None, tm, w), lambda b, i: (b, i, 0))
    zspec = pl.BlockSpec((None, SSM_GROUPS, tm // CHUNK, _ZW), lambda b, i: (b, 0, i, 0))
    zshape = jax.ShapeDtypeStruct((B, SSM_GROUPS, R // CHUNK, _ZW), BF16)
    return pl.pallas_call(
        functools.partial(_inproj_kernel, d_model=D),
        grid=(B, R // tm),
        in_specs=[row(D),
                  pl.BlockSpec((None, N_MOD, D), lambda b, i: (b, 0, 0)),
                  _resident((1, D)),
                  _resident(lw["w1"].shape),
                  _resident((1, MLA_Q_RANK)),
                  _resident((1, MLA_KV_RANK)),
                  _resident(lw["wq"].shape),
                  _resident(lw["wkv"].shape),
                  _resident(lw["e"].shape),
                  pl.BlockSpec((tm, 4 * LANES), lambda b, i: (i, 0)),
                  pl.BlockSpec((tm, 4 * LANES), lambda b, i: (i, 0))],
        out_specs=[zspec if w is None else row(w) for w in widths],
        out_shape=[zshape if w is None else jax.ShapeDtypeStruct((B, R, w), BF16) for w in widths],
        scratch_shapes=[pltpu.VMEM((SSM_WIDTH // LANES, tm, LANES), F32)],
        compiler_params=_params(("arbitrary", "arbitrary")),
        name="inproj",
    )(h, mod, g.reshape(1, D), lw["w1"], lw["qn"], lw["kvn"], lw["wq"], lw["wkv"], lw["e"], tabm, tabg)


def _mla_kernel(q_ref, kc_ref, vc_ref, *rest, tk, n_chunks):
    if n_chunks:
        k_ref, v_ref, o_ref = rest
    else:
        (o_ref,) = rest
    tq = q_ref.shape[0]

    def tile(kb, vb, carry):
        out = []
        for hh in range(2):
            m, l, acc = carry[hh]
            hs = slice(hh * HEAD_BLK, (hh + 1) * HEAD_BLK)
            s = _dot_nt(q_ref[:, hs], kb[:, hs])
            m_new = jnp.maximum(m, jnp.max(s, axis=-1, keepdims=True))
            alpha = jnp.exp2(m - m_new)
            p = jnp.exp2(s - m_new)
            l = alpha * l + jnp.sum(p, axis=-1, keepdims=True)
            acc = alpha * acc + _dot(p.astype(BF16), vb)
            out.append((m_new, l, acc))
        return tuple(out)

    init = tuple((jnp.full((tq, 1), NEG_INF, F32), jnp.zeros((tq, 1), F32), jnp.zeros((tq, LANES), F32))
                 for _ in range(2))
    carry = tile(kc_ref[...], vc_ref[...], init)
    if n_chunks:
        def body(j, c):
            r = pl.ds(pl.multiple_of(j * tk, tk), tk)
            return tile(k_ref[r, :], v_ref[r, :], c)
        carry = lax.fori_loop(0, n_chunks, body, carry)
    lane = lax.broadcasted_iota(jnp.int32, (tq, LANES), 1)
    o_ref[...] = jnp.where(lane < MLA_V, carry[0][2] / carry[0][1], carry[1][2] / carry[1][1]).astype(BF16)


def _mla(q, kc, vc, k=None, v=None):
    B, Tq, _ = q.shape
    C = kc.shape[1]
    tq = min(MLA_TQ, Tq)
    nh2 = MLA_HEADS // 2
    in_specs = [pl.BlockSpec((None, tq, 2 * HEAD_BLK), lambda b, h, i: (b, i, h)),
                pl.BlockSpec((None, C, 2 * HEAD_BLK), lambda b, h, i: (b, 0, h)),
                pl.BlockSpec((None, C, 2 * MLA_V), lambda b, h, i: (b, 0, h))]
    args = [q, kc, vc]
    n_chunks, tk = 0, 0
    if k is not None:
        T = k.shape[1]
        tk = min(MLA_TK, T)
        n_chunks = T // tk
        in_specs += [pl.BlockSpec((None, T, 2 * HEAD_BLK), lambda b, h, i: (b, 0, h)),
                     pl.BlockSpec((None, T, 2 * MLA_V), lambda b, h, i: (b, 0, h))]
        args += [k, v]
    return pl.pallas_call(
        functools.partial(_mla_kernel, tk=tk, n_chunks=n_chunks),
        grid=(B, nh2, Tq // tq),
        in_specs=in_specs,
        out_specs=pl.BlockSpec((None, tq, 2 * MLA_V), lambda b, h, i: (b, i, h)),
        out_shape=jax.ShapeDtypeStruct((B, Tq, _VW), BF16),
        compiler_params=_params(("arbitrary", "arbitrary", "arbitrary")),
        name="mla",
    )(*args)


def _gqa_kernel(sink_ref, q_ref, kc_ref, vc_ref, *rest, tq, band, seq):
    if band:
        k_ref, v_ref, o_ref = rest
    else:
        (o_ref,) = rest
    half = GQA_HEAD_DIM
    lane = lax.broadcasted_iota(jnp.int32, (tq, LANES), 1)
    kc = kc_ref[...]
    vc = vc_ref[...]
    if band:
        nk = tq + 2 * WINDOW
        q0 = pl.program_id(1) * tq
        start = pl.multiple_of(jnp.clip(q0 - WINDOW, 0, seq - nk), LANES)
        kw = k_ref[pl.ds(start, nk), :]
        vw = v_ref[pl.ds(start, nk), :]
        r = lax.broadcasted_iota(jnp.int32, (2 * tq, nk), 0)
        qpos = q0 + jnp.where(r < tq, r, r - tq)
        kpos = start + lax.broadcasted_iota(jnp.int32, (2 * tq, nk), 1)
        valid = jnp.abs(qpos - kpos) <= WINDOW
    rows = lax.broadcasted_iota(jnp.int32, (2 * tq, 1), 0)
    for j in range(GQA_HEADS // 2):
        qb = q_ref[:, j * LANES:(j + 1) * LANES]
        zero = jnp.zeros_like(qb)
        qq = jnp.concatenate([jnp.where(lane < half, qb, zero), jnp.where(lane >= half, qb, zero)], axis=0)
        sink = jnp.where(rows < tq, sink_ref[j], sink_ref[GQA_HEADS // 2 + j])
        s_c = _dot_nt(qq, kc)
        m = jnp.maximum(sink, jnp.max(s_c, axis=-1, keepdims=True))
        if band:
            s_b = jnp.where(valid, _dot_nt(qq, kw), NEG_INF)
            m = jnp.maximum(m, jnp.max(s_b, axis=-1, keepdims=True))
        p_c = jnp.exp2(s_c - m)
        den = jnp.exp2(sink - m) + jnp.sum(p_c, axis=-1, keepdims=True)
        o = _dot(p_c.astype(BF16), vc)
        if band:
            p_b = jnp.exp2(s_b - m)
            den = den + jnp.sum(p_b, axis=-1, keepdims=True)
            o = o + _dot(p_b.astype(BF16), vw)
        o = o / den
        o_ref[:, j * LANES:(j + 1) * LANES] = jnp.where(lane < half, o[:tq], o[tq:]).astype(BF16)


def _gqa(sink, q, kc, vc, k=None, v=None):
    B, Tq, _ = q.shape
    C = kc.shape[1]
    band = k is not None
    tq = min(256, Tq)
    full = lambda n: pl.BlockSpec((None, n, _GKW), lambda b, i: (b, 0, 0))
    in_specs = [pl.BlockSpec(memory_space=pltpu.SMEM),
                pl.BlockSpec((None, tq, _GQW), lambda b, i: (b, i, 0)),
                full(C), full(C)]
    args = [sink, q, kc, vc]
    seq = 0
    if band:
        seq = k.shape[1]
        assert seq >= tq + 2 * WINDOW
        in_specs += [full(seq), full(seq)]
        args += [k, v]
    return pl.pallas_call(
        functools.partial(_gqa_kernel, tq=tq, band=band, seq=seq),
        grid=(B, Tq // tq),
        in_specs=in_specs,
        out_specs=pl.BlockSpec((None, tq, _GQW), lambda b, i: (b, i, 0)),
        out_shape=jax.ShapeDtypeStruct((B, Tq, _GQW), BF16),
        compiler_params=_params(("arbitrary", "arbitrary")),
        name="gqa",
    )(*args)


def _gelu(y):
    return 0.5 * y * (1.0 + jnp.tanh(math.sqrt(2.0 / math.pi) * (y + 0.044715 * (y * y * y))))


def _ssm_kernel(zc_ref, zl_ref, we_ref, wy_ref, a_ref, yc_ref, yl_ref, e_ref, sf_ref, sr_ref,
                *, n_ctx, n_lat, batch):
    sw = PAIR * SSM_STATE
    n_all = n_ctx + n_lat
    segs = ((zc_ref, yc_ref, 0, n_ctx), (zl_ref, yl_ref, n_ctx, n_lat))

    def chunk_rows(c0, n, pp, b):
        return pl.ds(c0 * SUBLANES + pp * batch + b, n, stride=SUBLANES)

    def z_pair(z_ref, b, pp):
        return jnp.concatenate([z_ref[b, PAIR * pp], z_ref[b, PAIR * pp + 1]], axis=1)

    for z_ref, _, c0, n in segs:
        for pp in range(PAIR):
            for b in range(batch):
                e = _dot(z_pair(z_ref, b, pp), we_ref[pp])
                for k in range(4):
                    e_ref[k, chunk_rows(c0, n, pp, b), :] = e[:, k * sw:(k + 1) * sw]
    af_re, af_im, ar_re, ar_im = a_ref[0], a_ref[1], a_ref[2], a_ref[3]
    zero = jnp.zeros((SUBLANES, sw), F32)

    def rows_of(c):
        return pl.ds(pl.multiple_of(c * SUBLANES, SUBLANES), SUBLANES)

    def fwd(c, carry):
        s_re, s_im = carry
        r = rows_of(c)
        sf_ref[0, r, :] = s_re
        sf_ref[1, r, :] = s_im
        return (af_re * s_re - af_im * s_im + e_ref[0, r, :],
                af_re * s_im + af_im * s_re + e_ref[1, r, :])

    lax.fori_loop(0, n_all, fwd, (zero, zero))

    def rev_at(c, carry):
        s_re, s_im = carry
        r = rows_of(c)
        sr_ref[0, r, :] = s_re
        sr_ref[1, r, :] = s_im
        return (ar_re * s_re - ar_im * s_im + e_ref[2, r, :],
                ar_re * s_im + ar_im * s_re + e_ref[3, r, :])

    carry = lax.fori_loop(0, n_ctx, lambda i, c: rev_at(n_ctx - 1 - i, c), (zero, zero))
    lax.fori_loop(0, n_lat, lambda i, c: rev_at(n_all - 1 - i, c), carry)

    uw = PAIR * _ZW
    for z_ref, y_ref, c0, n in segs:
        for pp in range(PAIR):
            for b in range(batch):
                r = chunk_rows(c0, n, pp, b)
                s_f = jnp.concatenate([sf_ref[0, r, :], sf_ref[1, r, :]], axis=1).astype(BF16)
                s_r = jnp.concatenate([sr_ref[0, r, :], sr_ref[1, r, :]], axis=1).astype(BF16)
                y = (_dot(z_pair(z_ref, b, pp), wy_ref[pp, 0:uw, :])
                     + _dot(s_f, wy_ref[pp, uw:uw + 2 * sw, :])
                     + _dot(s_r, wy_ref[pp, uw + 2 * sw:uw + 4 * sw, :]))
                y = _gelu(y).astype(BF16)
                y_ref[b, PAIR * pp] = y[:, :_ZW]
                y_ref[b, PAIR * pp + 1] = y[:, _ZW:]


def _ssm(z_lat, z_ctx, we, wy, a):
    B, G, n_lat, _ = z_lat.shape
    n_ctx = z_ctx.shape[2]
    assert PAIR * B == SUBLANES, "scan rows pack (pair, batch) into one sublane tile"
    gstep = PAIR * PAIR
    rows = (n_ctx + n_lat) * SUBLANES
    sw2 = 2 * PAIR * SSM_STATE
    uw = PAIR * _ZW
    zspec = lambda n: pl.BlockSpec((B, gstep, n, _ZW), lambda g: (0, g, 0, 0))
    y_ctx, y_lat = pl.pallas_call(
        functools.partial(_ssm_kernel, n_ctx=n_ctx, n_lat=n_lat, batch=B),
        grid=(G // gstep,),
        in_specs=[zspec(n_ctx), zspec(n_lat),
                  pl.BlockSpec((PAIR, uw, 2 * sw2), lambda g: (g, 0, 0)),
                  pl.BlockSpec((PAIR, uw + 2 * sw2, uw), lambda g: (g, 0, 0)),
                  pl.BlockSpec((None, 4, SUBLANES, PAIR * SSM_STATE), lambda g: (g, 0, 0, 0))],
        out_specs=[zspec(n_ctx), zspec(n_lat)],
        out_shape=[jax.ShapeDtypeStruct(z_ctx.shape, BF16), jax.ShapeDtypeStruct(z_lat.shape, BF16)],
        scratch_shapes=[pltpu.VMEM((4, rows, PAIR * SSM_STATE), F32),
                        pltpu.VMEM((2, rows, PAIR * SSM_STATE), F32),
                        pltpu.VMEM((2, rows, PAIR * SSM_STATE), F32)],
        compiler_params=_params(("arbitrary",)),
        name="ssm",
    )(z_ctx, z_lat, we, wy, a)
    return y_lat, y_ctx


def _merge_kernel(h_ref, mod_ref, a_ref, yg_ref, g_ref, gate_ref, wo_ref, wglu_ref, wgo_ref, wout_ref, o_ref,
                  ys_ref, *, d_model):
    D = d_model
    b0 = _dot(a_ref[...], wo_ref[...])
    nr = ys_ref.shape[1] // CHUNK
    masks = _piece_masks(nr)
    for lt in range(SSM_WIDTH // LANES):
        for jh in range(_ZW // LANES):
            a = [yg_ref[lt * _PIECES + s, :, jh * LANES:(jh + 1) * LANES].astype(F32) for s in range(_PIECES)]
            b = _block_transpose(a, masks)
            for d in range(_PIECES):
                ys_ref[lt, pl.ds(jh * _PIECES + d, nr, stride=CHUNK), :] = b[d]
    ys = jnp.concatenate([ys_ref[lt] for lt in range(SSM_WIDTH // LANES)], axis=1)
    t = _dot(ys.astype(BF16), wglu_ref[...])
    b1 = t[:, :D] * _sigmoid(t[:, D:])
    b2 = _dot(g_ref[...], wgo_ref[...])
    gate = gate_ref[...].astype(F32)
    mix = gate[:, 0:D] * b0 + gate[:, D:2 * D] * b1 + gate[:, 2 * D:3 * D] * b2
    o_ref[...] = h_ref[...] + mod_ref[5:6, :] * _dot(mix.astype(BF16), wout_ref[...])


def _merge(h, mod, attn, yssm, gqa, gate, lw):
    B, R, D = h.shape
    tm = min(512, R)
    row = lambda w: pl.BlockSpec((None, tm, w), lambda b, i: (b, i, 0))
    return pl.pallas_call(
        functools.partial(_merge_kernel, d_model=D),
        grid=(B, R // tm),
        in_specs=[row(D),
                  pl.BlockSpec((None, N_MOD, D), lambda b, i: (b, 0, 0)),
                  row(_VW),
                  pl.BlockSpec((None, SSM_GROUPS, tm // CHUNK, _ZW), lambda b, i: (b, 0, i, 0)),
                  row(_GQW), row(N_BRANCH * D),
                  _resident(lw["wo"].shape), _resident(lw["wglu"].shape),
                  _resident(lw["wgo"].shape), _resident(lw["wout"].shape)],
        out_specs=row(D),
        out_shape=jax.ShapeDtypeStruct((B, R, D), F32),
        scratch_shapes=[pltpu.VMEM((SSM_WIDTH // LANES, tm, LANES), F32)],
        compiler_params=_params(("arbitrary", "arbitrary")),
        name="merge",
    )(h, mod, attn, yssm, gqa, gate, lw["wo"], lw["wglu"], lw["wgo"], lw["wout"])


def _rot_index(width, n_axial):
    h = n_axial // 2
    src = np.arange(width)
    sign = np.ones(width, np.float32)
    for base in range(0, width, n_axial):
        for i in range(h):
            src[base + i], sign[base + i] = base + i + h, -1.0
            src[base + i + h], sign[base + i + h] = base + i, 1.0
    return src, sign


def _rot_cols(w, n_axial):
    src, sign = _rot_index(w.shape[-1], n_axial)
    return w[..., src] * sign


def _rope_angles(T, n):
    t = np.arange(T)
    inv = ROPE_BASE ** (-np.arange(0, n, 2, dtype=np.float32) / n)
    out = []
    for pos in (t // GRID_W, t % GRID_W):
        ang = jnp.asarray(pos.astype(np.float32))[:, None] * jnp.asarray(inv)[None, :]
        out.append((jnp.cos(ang), jnp.sin(ang)))
    cos = jnp.concatenate([out[0][0], out[0][0], out[1][0], out[1][0]], axis=1)
    sin = jnp.concatenate([out[0][1], out[0][1], out[1][1], out[1][1]], axis=1)
    return cos, sin


def _tables(T, C):
    sc_m = (MLA_NOPE + MLA_ROPE) ** -0.5 * LOG2E
    sc_g = GQA_HEAD_DIM ** -0.5 * LOG2E
    cm, sm = _rope_angles(T, MLA_ROPE // 2)
    cg, sg = _rope_angles(T, GQA_HEAD_DIM // 2)
    one = lambda n, w: jnp.ones((n, w), F32)
    zero = lambda n, w: jnp.zeros((n, w), F32)
    pad = LANES - MLA_NOPE - MLA_ROPE

    def mla_tab(n, c, s):
        cos_q = jnp.concatenate([one(n, MLA_NOPE), c, zero(n, pad)], axis=1) * sc_m
        sin_q = jnp.concatenate([zero(n, MLA_NOPE), s, zero(n, pad)], axis=1) * sc_m
        kr = jnp.concatenate([c, s, zero(n, LANES - 2 * MLA_ROPE)], axis=1)
        return jnp.concatenate([cos_q, sin_q, kr, zero(n, LANES)], axis=1)

    def gqa_tab(n, c, s):
        c2, s2 = jnp.concatenate([c, c], axis=1), jnp.concatenate([s, s], axis=1)
        return jnp.concatenate([c2 * sc_g, s2 * sc_g, c2, s2], axis=1)

    lat = (mla_tab(T, cm, sm), gqa_tab(T, cg, sg))
    ctx = (mla_tab(C, one(C, MLA_ROPE), zero(C, MLA_ROPE)),
           gqa_tab(C, one(C, GQA_HEAD_DIM), zero(C, GQA_HEAD_DIM)))
    return lat, ctx


def _place_matrix():
    e = np.zeros((LANES, _QW), np.float32)
    for h in range(MLA_HEADS):
        for j in range(MLA_ROPE):
            e[j, h * HEAD_BLK + MLA_NOPE + j] = 1.0
            e[MLA_ROPE + j, h * HEAD_BLK + MLA_NOPE + j] = 1.0
    return jnp.asarray(e, BF16)


def _layer_weights(w_in, q_norm, kv_norm, w_uq, w_ukv, w_o, w_glu, gqa_w_o, w_out):
    D = w_in.shape[0]
    offs = np.cumsum([MLA_Q_RANK, MLA_KV_RANK, MLA_ROPE, SSM_WIDTH, _GQW, _GKW, _GKW]).tolist()
    cq, ckv, kr, u, gq, gk, gv, gates = jnp.split(w_in, offs, axis=1)
    order = np.asarray(GQA_ORDER)
    gq_h = gq.reshape(D, GQA_HEADS, GQA_HEAD_DIM)
    gq_r = _rot_cols(gq_h, GQA_HEAD_DIM // 2)
    gk_r = _rot_cols(gk.reshape(D, GQA_KV_HEADS, GQA_HEAD_DIM), GQA_HEAD_DIM // 2).reshape(D, _GKW)
    kr_blk = jnp.concatenate([kr, _rot_cols(kr, MLA_ROPE // 2), jnp.zeros((D, LANES - 2 * MLA_ROPE), F32)], axis=1)
    w1 = jnp.concatenate([cq, ckv, u, gq_h[:, order].reshape(D, _GQW), gq_r[:, order].reshape(D, _GQW),
                          gk, gk_r, gv, gates, kr_blk], axis=1).astype(BF16)
    uq = w_uq.reshape(MLA_Q_RANK, MLA_HEADS, MLA_NOPE + MLA_ROPE)
    nope, rope = uq[..., :MLA_NOPE], uq[..., MLA_NOPE:]
    zpad = jnp.zeros((MLA_Q_RANK, MLA_HEADS, HEAD_BLK - MLA_NOPE - MLA_ROPE), F32)
    wq_a = jnp.concatenate([nope, rope, zpad], axis=-1).reshape(MLA_Q_RANK, _QW)
    wq_b = jnp.concatenate([jnp.zeros_like(nope), _rot_cols(rope, MLA_ROPE // 2), zpad], axis=-1)
    wq = jnp.concatenate([wq_a, wq_b.reshape(MLA_Q_RANK, _QW)], axis=1).astype(BF16)
    ukv = w_ukv.reshape(MLA_KV_RANK, MLA_HEADS, MLA_NOPE + MLA_V)
    wk = jnp.concatenate([ukv[..., :MLA_NOPE], jnp.zeros((MLA_KV_RANK, MLA_HEADS, HEAD_BLK - MLA_NOPE), F32)],
                         axis=-1).reshape(MLA_KV_RANK, _QW)
    wv = ukv[..., MLA_NOPE:].reshape(MLA_KV_RANK, _VW)
    wkv = jnp.concatenate([wk, wv], axis=1).astype(BF16)
    wgo = gqa_w_o.reshape(GQA_HEADS, GQA_HEAD_DIM, D)[order].reshape(_GQW, D).astype(BF16)
    return dict(w1=w1, qn=q_norm.reshape(1, -1), kvn=kv_norm.reshape(1, -1), wq=wq, wkv=wkv, e=_place_matrix(),
                wo=w_o.astype(BF16), wglu=w_glu.astype(BF16), wgo=wgo, wout=w_out.astype(BF16))


def _ssm_weights(lam_re, lam_im, log_dt, b_re, b_im, c_re, c_im, d_skip):
    hp = lax.Precision.HIGHEST
    G, P, M, Lc = SSM_GROUPS, SSM_STATE, SSM_GROUP, CHUNK
    dt = jnp.exp(log_dt)[..., None]
    kk = jnp.arange(Lc + 1, dtype=F32)[:, None, None, None]
    mag = jnp.exp(lam_re[None] * dt[None] * kk)
    pw_re, pw_im = mag * jnp.cos(lam_im[None] * dt[None] * kk), mag * jnp.sin(lam_im[None] * dt[None] * kk)
    a_re, a_im = pw_re[1], pw_im[1]
    den = lam_re * lam_re + lam_im * lam_im
    w_re = ((a_re - 1) * lam_re + a_im * lam_im) / den
    w_im = (a_im * lam_re - (a_re - 1) * lam_im) / den
    bb_re = w_re[..., None] * b_re - w_im[..., None] * b_im
    bb_im = w_re[..., None] * b_im + w_im[..., None] * b_re
    ca_re = c_re[:, None] * jnp.moveaxis(pw_re, 0, 1)[:, :, :, None, :] - c_im[:, None] * jnp.moveaxis(pw_im, 0, 1)[:, :, :, None, :]
    ca_im = c_re[:, None] * jnp.moveaxis(pw_im, 0, 1)[:, :, :, None, :] + c_im[:, None] * jnp.moveaxis(pw_re, 0, 1)[:, :, :, None, :]
    cb = (jnp.einsum('dkgop,dgpm->dkgom', ca_re, bb_re, precision=hp)
          - jnp.einsum('dkgop,dgpm->dkgom', ca_im, bb_im, precision=hp))
    lag = np.arange(Lc)[None, :] - np.arange(Lc)[:, None]
    kf = jnp.where((lag >= 0)[None, :, :, None, None], cb[0][np.clip(lag, 0, Lc)].transpose(2, 0, 1, 3, 4), 0.0)
    kr = jnp.where((lag <= 0)[None, :, :, None, None], cb[1][np.clip(-lag, 0, Lc)].transpose(2, 0, 1, 3, 4), 0.0)
    kmat = kf + kr
    skip = d_skip.reshape(G, M)
    eye = jnp.eye(Lc, dtype=F32)[None, :, :, None, None] * (jnp.eye(M, dtype=F32)[None, None, None] * skip[:, None, None, :, None])
    kmat = jnp.transpose(kmat + eye, (0, 1, 4, 2, 3)).reshape(G, Lc * M, Lc * M)
    idx_f = np.arange(Lc)[::-1].copy()
    idx_r = np.arange(Lc)

    def drive(d, idx):
        p_re, p_im = pw_re[idx, d], pw_im[idx, d]
        e_re = p_re[..., None] * bb_re[d][None] - p_im[..., None] * bb_im[d][None]
        e_im = p_re[..., None] * bb_im[d][None] + p_im[..., None] * bb_re[d][None]
        to = lambda t: jnp.transpose(t, (1, 0, 3, 2)).reshape(G, Lc * M, P)
        return to(e_re), to(e_im)

    ef_re, ef_im = drive(0, idx_f)
    er_re, er_im = drive(1, idx_r)

    def readout(d, idx):
        q_re, q_im = ca_re[d][idx], ca_im[d][idx]
        to = lambda t: jnp.transpose(t, (1, 3, 0, 2)).reshape(G, P, Lc * M)
        return to(q_re), to(-q_im)

    qf_re, qf_im = readout(0, np.arange(1, Lc + 1))
    qr_re, qr_im = readout(1, Lc - np.arange(Lc))
    npair = G // PAIR

    def pair_diag(t):
        R, Cn = t.shape[1:]
        t = t.reshape(npair, PAIR, R, Cn)
        out = jnp.zeros((npair, PAIR, R, PAIR, Cn), F32)
        for p in range(PAIR):
            out = out.at[:, p, :, p, :].set(t[:, p])
        return out.reshape(npair, PAIR * R, PAIR * Cn)

    we = jnp.concatenate([pair_diag(ef_re), pair_diag(ef_im), pair_diag(er_re), pair_diag(er_im)], axis=2)
    wy = jnp.concatenate([pair_diag(kmat), pair_diag(qf_re), pair_diag(qf_im), pair_diag(qr_re), pair_diag(qr_im)], axis=1)
    sw = PAIR * P
    nstep = npair // PAIR

    def per_row(t):
        t = t.reshape(nstep, PAIR, 1, sw)
        return jnp.broadcast_to(t, (nstep, PAIR, SUBLANES // PAIR, sw)).reshape(nstep, SUBLANES, sw)

    a_step = jnp.stack([per_row(pw_re[Lc, 0]), per_row(pw_im[Lc, 0]), per_row(pw_re[Lc, 1]), per_row(pw_im[Lc, 1])],
                       axis=1)
    return we.astype(BF16), wy.astype(BF16), a_step


def kernel(x, c, ctx, c_ctx, ada_w, ada_b, norm_ffn1, norm_mix, norm_ffn2, ffn1_w13, ffn1_w2, ffn2_w13, ffn2_w2, w_in, mla_q_norm, mla_kv_norm, mla_w_uq, mla_w_ukv, mla_w_o, ssm_lambda_re, ssm_lambda_im, ssm_log_dt, ssm_b_re, ssm_b_im, ssm_c_re, ssm_c_im, ssm_d, ssm_w_glu, gqa_sink, gqa_w_o, w_out, final_norm):
    B, T, D = x.shape
    C = ctx.shape[1]
    depth = ada_w.shape[0]
    cc = jnp.concatenate([c, c_ctx[None], jnp.zeros((SUBLANES - B - 1, D), F32)], axis=0)
    mods = _ada(cc, ada_w, ada_b).reshape(depth, SUBLANES, N_MOD, D)
    (tabm, tabg), (tabm_c, tabg_c) = _tables(T, C)
    sinks = gqa_sink * LOG2E
    h, hc = x, ctx
    for l in range(depth):
        ctx_out = l < depth - 1
        last = l == depth - 1
        mod = mods[l, :B]
        mod_c = jnp.broadcast_to(mods[l, B][None], (B, N_MOD, D))
        lw = _layer_weights(w_in[l], mla_q_norm[l], mla_kv_norm[l], mla_w_uq[l], mla_w_ukv[l], mla_w_o[l],
                            ssm_w_glu[l], gqa_w_o[l], w_out[l])
        we, wy, a_step = _ssm_weights(ssm_lambda_re[l], ssm_lambda_im[l], ssm_log_dt[l], ssm_b_re[l], ssm_b_im[l],
                                      ssm_c_re[l], ssm_c_im[l], ssm_d[l])
        w13a, w2a = ffn1_w13[l].astype(BF16), ffn1_w2[l].astype(BF16)
        h = _ffn(h, mod, norm_ffn1[l], w13a, w2a, 0)
        hc = _ffn(hc, mod_c, norm_ffn1[l], w13a, w2a, 0)
        q, k, v, z, gq, gk, gv, gate = _inproj(h, mod, norm_mix[l], lw, tabm, tabg)
        q_c, k_c, v_c, z_c, gq_c, gk_c, gv_c, gate_c = _inproj(hc, mod_c, norm_mix[l], lw, tabm_c, tabg_c)
        attn = _mla(q, k_c, v_c, k, v)
        y_lat, y_ctx = _ssm(z, z_c, we, wy, a_step)
        gqa = _gqa(sinks[l], gq, gk_c, gv_c, gk, gv)
        h = _merge(h, mod, attn, y_lat, gqa, gate, lw)
        w13b, w2b = ffn2_w13[l].astype(BF16), ffn2_w2[l].astype(BF16)
        h = _ffn(h, mod, norm_ffn2[l], w13b, w2b, 6, final_norm if last else None)
        if ctx_out:
            attn_c = _mla(q_c, k_c, v_c)
            gqa_c = _gqa(sinks[l], gq_c, gk_c, gv_c)
            hc = _merge(hc, mod_c, attn_c, y_ctx, gqa_c, gate_c, lw)
            hc = _ffn(hc, mod_c, norm_ffn2[l], w13b, w2b, 6)
    return h
```

```python
import functools
import math

import numpy as np
import jax
import jax.numpy as jnp
from jax import lax
from jax.experimental import pallas as pl
from jax.experimental.pallas import tpu as pltpu

F32 = jnp.float32
BF16 = jnp.bfloat16

GRID_W = 64
MLA_HEADS = 8
MLA_NOPE = 64
MLA_ROPE = 32
MLA_V = 64
MLA_Q_RANK = 384
MLA_KV_RANK = 256
SSM_WIDTH = 512
SSM_GROUP = 16
SSM_GROUPS = SSM_WIDTH // SSM_GROUP
SSM_STATE = 64
GQA_HEADS = 8
GQA_KV_HEADS = 2
GQA_HEAD_DIM = 64
WINDOW = 128
N_BRANCH = 3
N_MOD = 9
ROPE_BASE = 10000.0
EPS = 1e-6
NEG_INF = -1e30
LOG2E = math.log2(math.e)

LANES = 128
SUBLANES = 8
VMEM_LIMIT = 56 * 1024 * 1024
CHUNK = 16
PAIR = 2
HEAD_BLK = LANES
MLA_TQ = 1024
MLA_TK = 2048
GQA_ORDER = (0, 4, 1, 5, 2, 6, 3, 7)


def _dot(a, b):
    return jnp.dot(a, b, preferred_element_type=F32)


def _dot_nt(a, b):
    return lax.dot_general(a, b, (((1,), (1,)), ((), ())), preferred_element_type=F32)


def _rms(x, g):
    return x * lax.rsqrt(jnp.mean(x * x, axis=-1, keepdims=True) + EPS) * g


def _sigmoid(x):
    return 1.0 / (1.0 + jnp.exp(-x))


def _params(sem, vmem=VMEM_LIMIT):
    return pltpu.CompilerParams(dimension_semantics=sem, vmem_limit_bytes=vmem)


def _resident(shape):
    nd = len(shape)
    return pl.BlockSpec(shape, lambda *_: (0,) * nd, pipeline_mode=pl.Buffered(1))


def _layer(arr, l):
    nd = arr.ndim
    return pl.BlockSpec((None,) + arr.shape[1:], lambda *_: (l,) + (0,) * (nd - 1), pipeline_mode=pl.Buffered(1))


def _mod_spec(mods, l, mrow):
    blk = (None, None) + mods.shape[2:]
    if mrow is None:
        return pl.BlockSpec(blk, lambda b, i: (l, b, 0, 0))
    return pl.BlockSpec(blk, lambda b, i: (l, mrow, 0, 0))


def _ada_kernel(c_ref, w_ref, b_ref, o_ref):
    c = c_ref[...]
    s = c * _sigmoid(c)
    w = w_ref[...]
    s_hi = s.astype(BF16)
    s_lo = (s - s_hi.astype(F32)).astype(BF16)
    w_hi = w.astype(BF16)
    w_lo = (w - w_hi.astype(F32)).astype(BF16)
    o_ref[...] = _dot(s_hi, w_hi) + _dot(s_hi, w_lo) + _dot(s_lo, w_hi) + b_ref[...]


def _ada(cc, ada_w, ada_b):
    L, D, N = ada_w.shape
    tn = 1152 if N % 1152 == 0 else N
    R = cc.shape[0]
    return pl.pallas_call(
        _ada_kernel,
        grid=(L, N // tn),
        in_specs=[pl.BlockSpec((R, D), lambda l, j: (0, 0)),
                  pl.BlockSpec((None, D, tn), lambda l, j: (l, 0, j)),
                  pl.BlockSpec((None, 1, tn), lambda l, j: (l, 0, j))],
        out_specs=pl.BlockSpec((None, R, tn), lambda l, j: (l, 0, j)),
        out_shape=jax.ShapeDtypeStruct((L, R, N), F32),
        compiler_params=_params(("arbitrary", "arbitrary")),
        name="ada",
    )(cc, ada_w, ada_b.reshape(L, 1, N))


def _ffn_kernel(h_ref, mod_ref, g_ref, w13_ref, w2_ref, *rest, k0, ff, tf, final):
    if final:
        fg_ref, o_ref = rest
    else:
        (o_ref,) = rest
    x = h_ref[...]
    mod = mod_ref[...]
    xn = _rms(x, g_ref[...]) * (1.0 + mod[k0 + 1:k0 + 2]) + mod[k0:k0 + 1]
    xb = xn.astype(BF16)
    acc = None
    for f0 in range(0, ff, tf):
        a = _dot(xb, w13_ref[:, f0:f0 + tf])
        b = _dot(xb, w13_ref[:, ff + f0:ff + f0 + tf])
        act = (a * _sigmoid(a) * b).astype(BF16)
        part = _dot(act, w2_ref[f0:f0 + tf, :])
        acc = part if acc is None else acc + part
    out = x + (0.5 * mod[k0 + 2:k0 + 3]) * acc
    if final:
        out = _rms(out, fg_ref[...])
    o_ref[...] = out


def _ffn(h, mods, l, mrow, g, w13, w2, k0, final_g=None):
    B, R, D = h.shape
    ff = w2.shape[1]
    tm = min(512, R)
    tf = ff // 2 if (ff // 2) % LANES == 0 else ff
    final = final_g is not None
    in_specs = [pl.BlockSpec((None, tm, D), lambda b, i: (b, i, 0)),
                _mod_spec(mods, l, mrow),
                _layer(g, l), _layer(w13, l), _layer(w2, l)]
    args = [h, mods, g, w13, w2]
    if final:
        in_specs.append(_resident((1, D)))
        args.append(final_g.reshape(1, D))
    return pl.pallas_call(
        functools.partial(_ffn_kernel, k0=k0, ff=ff, tf=tf, final=final),
        grid=(B, R // tm),
        in_specs=in_specs,
        out_specs=pl.BlockSpec((None, tm, D), lambda b, i: (b, i, 0)),
        out_shape=jax.ShapeDtypeStruct((B, R, D), F32),
        compiler_params=_params(("arbitrary", "arbitrary")),
        name="ffn",
    )(*args)


_O_CQ = 0
_O_CKV = _O_CQ + MLA_Q_RANK
_O_U = _O_CKV + MLA_KV_RANK
_O_GQ = _O_U + SSM_WIDTH
_O_GQR = _O_GQ + GQA_HEADS * GQA_HEAD_DIM
_O_GK = _O_GQR + GQA_HEADS * GQA_HEAD_DIM
_O_GKR = _O_GK + GQA_KV_HEADS * GQA_HEAD_DIM
_O_GV = _O_GKR + GQA_KV_HEADS * GQA_HEAD_DIM
_O_GATE = _O_GV + GQA_KV_HEADS * GQA_HEAD_DIM
_O_KR = _O_GATE + N_BRANCH * 1024
_W1_COLS = _O_KR + LANES
_QW = MLA_HEADS * HEAD_BLK
_VW = MLA_HEADS * MLA_V
_GQW = GQA_HEADS * GQA_HEAD_DIM
_GKW = GQA_KV_HEADS * GQA_HEAD_DIM


_PIECES = LANES // SSM_GROUP
_ZW = CHUNK * SSM_GROUP


def _piece_masks(rows):
    lane = lax.broadcasted_iota(jnp.int32, (rows, LANES), 1)
    return [(lane >= i * SSM_GROUP) & (lane < (i + 1) * SSM_GROUP) for i in range(_PIECES)]


def _block_transpose(a, masks):
    out = []
    for d in range(_PIECES):
        acc = None
        for s in range(_PIECES):
            shift = ((s - d) % _PIECES) * SSM_GROUP
            r = pltpu.roll(a[s], shift, 1) if shift else a[s]
            acc = r if acc is None else jnp.where(masks[s], r, acc)
        out.append(acc)
    return out


def _inproj_kernel(h_ref, mod_ref, g_ref, w1_ref, qn_ref, kvn_ref, wq_ref, wkv_ref, e_ref, tabm_ref, tabg_ref,
                   q_ref, k_ref, v_ref, z_ref, gq_ref, gk_ref, gv_ref, gate_ref, us_ref, *, d_model):
    x = h_ref[...]
    mod = mod_ref[...]
    xb = (_rms(x, g_ref[...]) * (1.0 + mod[4:5]) + mod[3:4]).astype(BF16)

    def proj(o, n):
        return _dot(xb, w1_ref[:, o:o + n])

    cqn = _rms(proj(_O_CQ, MLA_Q_RANK), qn_ref[...]).astype(BF16)
    qq = _dot(cqn, wq_ref[...])
    cos_q = tabm_ref[:, 0:LANES]
    sin_q = tabm_ref[:, LANES:2 * LANES]
    for h in range(MLA_HEADS):
        a = qq[:, h * HEAD_BLK:(h + 1) * HEAD_BLK]
        b = qq[:, _QW + h * HEAD_BLK:_QW + (h + 1) * HEAD_BLK]
        q_ref[:, h * HEAD_BLK:(h + 1) * HEAD_BLK] = (a * cos_q + b * sin_q).astype(BF16)
    ckvn = _rms(proj(_O_CKV, MLA_KV_RANK), kvn_ref[...]).astype(BF16)
    kr = (proj(_O_KR, LANES) * tabm_ref[:, 2 * LANES:3 * LANES]).astype(BF16)
    k_ref[...] = (_dot(ckvn, wkv_ref[:, 0:_QW]) + _dot(kr, e_ref[...])).astype(BF16)
    v_ref[...] = _dot(ckvn, wkv_ref[:, _QW:_QW + _VW]).astype(BF16)
    u = proj(_O_U, SSM_WIDTH)
    nr = us_ref.shape[1] // CHUNK
    masks = _piece_masks(nr)
    for lt in range(SSM_WIDTH // LANES):
        us_ref[lt] = u[:, lt * LANES:(lt + 1) * LANES]
        for jh in range(_ZW // LANES):
            a = [us_ref[lt, pl.ds(jh * _PIECES + s, nr, stride=CHUNK), :] for s in range(_PIECES)]
            b = _block_transpose(a, masks)
            for d in range(_PIECES):
                z_ref[lt * _PIECES + d, :, jh * LANES:(jh + 1) * LANES] = b[d].astype(BF16)
    cos_gq = tabg_ref[:, 0:LANES]
    sin_gq = tabg_ref[:, LANES:2 * LANES]
    cos_gk = tabg_ref[:, 2 * LANES:3 * LANES]
    sin_gk = tabg_ref[:, 3 * LANES:4 * LANES]
    gq = proj(_O_GQ, _GQW)
    gqr = proj(_O_GQR, _GQW)
    for j in range(_GQW // LANES):
        sl = slice(j * LANES, (j + 1) * LANES)
        gq_ref[:, sl] = (gq[:, sl] * cos_gq + gqr[:, sl] * sin_gq).astype(BF16)
    gk_ref[...] = (proj(_O_GK, _GKW) * cos_gk + proj(_O_GKR, _GKW) * sin_gk).astype(BF16)
    gv_ref[...] = proj(_O_GV, _GKW).astype(BF16)
    gate_ref[...] = _sigmoid(proj(_O_GATE, N_BRANCH * d_model)).astype(BF16)


def _inproj(h, mods, l, mrow, g, lw, e_mat, tabm, tabg):
    B, R, D = h.shape
    tm = min(512, R)
    widths = (_QW, _QW, _VW, None, _GQW, _GKW, _GKW, N_BRANCH * D)
    row = lambda w: pl.BlockSpec((None, tm, w), lambda b, i: (b, i, 0))
    zspec = pl.BlockSpec((None, SSM_GROUPS, tm // CHUNK, _ZW), lambda b, i: (b, 0, i, 0))
    zshape = jax.ShapeDtypeStruct((B, SSM_GROUPS, R // CHUNK, _ZW), BF16)
    return pl.pallas_call(
        functools.partial(_inproj_kernel, d_model=D),
        grid=(B, R // tm),
        in_specs=[row(D),
                  _mod_spec(mods, l, mrow),
                  _layer(g, l), _layer(lw["w1"], l), _layer(lw["qn"], l), _layer(lw["kvn"], l),
                  _layer(lw["wq"], l), _layer(lw["wkv"], l),
                  _resident(e_mat.shape),
                  pl.BlockSpec((tm, 4 * LANES), lambda b, i: (i, 0)),
                  pl.BlockSpec((tm, 4 * LANES), lambda b, i: (i, 0))],
        out_specs=[zspec if w is None else row(w) for w in widths],
        out_shape=[zshape if w is None else jax.ShapeDtypeStruct((B, R, w), BF16) for w in widths],
        scratch_shapes=[pltpu.VMEM((SSM_WIDTH // LANES, tm, LANES), F32)],
        compiler_params=_params(("arbitrary", "arbitrary")),
        name="inproj",
    )(h, mods, g, lw["w1"], lw["qn"], lw["kvn"], lw["wq"], lw["wkv"], e_mat, tabm, tabg)


def _mla_kernel(q_ref, kc_ref, vc_ref, *rest, tk, n_chunks):
    if n_chunks:
        k_ref, v_ref, o_ref = rest
    else:
        (o_ref,) = rest
    tq = q_ref.shape[0]

    def tile(kb, vb, carry):
        lane_v = lax.broadcasted_iota(jnp.int32, vb.shape, 1)
        one = jnp.ones_like(vb)
        vbs = (jnp.where(lane_v < MLA_V, vb, one), jnp.where(lane_v >= MLA_V, vb, one))
        out = []
        for hh in range(2):
            m, acc = carry[hh]
            hs = slice(hh * HEAD_BLK, (hh + 1) * HEAD_BLK)
            s = _dot_nt(q_ref[:, hs], kb[:, hs])
            m_new = jnp.maximum(m, jnp.max(s, axis=-1, keepdims=True))
            alpha = jnp.exp2(m - m_new)
            p = jnp.exp2(s - m_new)
            acc = alpha * acc + _dot(p.astype(BF16), vbs[hh])
            out.append((m_new, acc))
        return tuple(out)

    init = tuple((jnp.full((tq, 1), NEG_INF, F32), jnp.zeros((tq, LANES), F32)) for _ in range(2))
    carry = tile(kc_ref[...], vc_ref[...], init)
    if n_chunks:
        def body(j, c):
            r = pl.ds(pl.multiple_of(j * tk, tk), tk)
            return tile(k_ref[r, :], v_ref[r, :], c)
        carry = lax.fori_loop(0, n_chunks, body, carry)
    lane = lax.broadcasted_iota(jnp.int32, (tq, LANES), 1)
    a0, a1 = carry[0][1], carry[1][1]
    o_ref[...] = jnp.where(lane < MLA_V, a0 / pltpu.roll(a0, MLA_V, 1), a1 / pltpu.roll(a1, MLA_V, 1)).astype(BF16)


def _mla(q, kc, vc, k=None, v=None):
    B, Tq, _ = q.shape
    C = kc.shape[1]
    tq = min(MLA_TQ, Tq)
    nh2 = MLA_HEADS // 2
    in_specs = [pl.BlockSpec((None, tq, 2 * HEAD_BLK), lambda b, h, i: (b, i, h)),
                pl.BlockSpec((None, C, 2 * HEAD_BLK), lambda b, h, i: (b, 0, h)),
                pl.BlockSpec((None, C, 2 * MLA_V), lambda b, h, i: (b, 0, h))]
    args = [q, kc, vc]
    n_chunks, tk = 0, 0
    if k is not None:
        T = k.shape[1]
        tk = min(MLA_TK, T)
        n_chunks = T // tk
        in_specs += [pl.BlockSpec((None, T, 2 * HEAD_BLK), lambda b, h, i: (b, 0, h)),
                     pl.BlockSpec((None, T, 2 * MLA_V), lambda b, h, i: (b, 0, h))]
        args += [k, v]
    return pl.pallas_call(
        functools.partial(_mla_kernel, tk=tk, n_chunks=n_chunks),
        grid=(B, nh2, Tq // tq),
        in_specs=in_specs,
        out_specs=pl.BlockSpec((None, tq, 2 * MLA_V), lambda b, h, i: (b, i, h)),
        out_shape=jax.ShapeDtypeStruct((B, Tq, _VW), BF16),
        compiler_params=_params(("arbitrary", "arbitrary", "arbitrary")),
        name="mla",
    )(*args)


def _gqa_kernel(sink_ref, q_ref, kc_ref, vc_ref, *rest, tq, band, seq, layer):
    if band:
        k_ref, v_ref, o_ref = rest
    else:
        (o_ref,) = rest
    half = GQA_HEAD_DIM
    lane = lax.broadcasted_iota(jnp.int32, (tq, LANES), 1)
    kc = kc_ref[...]
    vc = vc_ref[...]
    if band:
        nk = tq + 2 * WINDOW
        q0 = pl.program_id(1) * tq
        start = pl.multiple_of(jnp.clip(q0 - WINDOW, 0, seq - nk), LANES)
        kw = k_ref[pl.ds(start, nk), :]
        vw = v_ref[pl.ds(start, nk), :]
        r = lax.broadcasted_iota(jnp.int32, (2 * tq, nk), 0)
        qpos = q0 + jnp.where(r < tq, r, r - tq)
        kpos = start + lax.broadcasted_iota(jnp.int32, (2 * tq, nk), 1)
        valid = jnp.abs(qpos - kpos) <= WINDOW
    rows = lax.broadcasted_iota(jnp.int32, (2 * tq, 1), 0)
    for j in range(GQA_HEADS // 2):
        qb = q_ref[:, j * LANES:(j + 1) * LANES]
        zero = jnp.zeros_like(qb)
        qq = jnp.concatenate([jnp.where(lane < half, qb, zero), jnp.where(lane >= half, qb, zero)], axis=0)
        sink = jnp.where(rows < tq, sink_ref[layer, j], sink_ref[layer, GQA_HEADS // 2 + j])
        s_c = _dot_nt(qq, kc)
        m = jnp.maximum(sink, jnp.max(s_c, axis=-1, keepdims=True))
        if band:
            s_b = jnp.where(valid, _dot_nt(qq, kw), NEG_INF)
            m = jnp.maximum(m, jnp.max(s_b, axis=-1, keepdims=True))
        p_c = jnp.exp2(s_c - m)
        den = jnp.exp2(sink - m) + jnp.sum(p_c, axis=-1, keepdims=True)
        o = _dot(p_c.astype(BF16), vc)
        if band:
            p_b = jnp.exp2(s_b - m)
            den = den + jnp.sum(p_b, axis=-1, keepdims=True)
            o = o + _dot(p_b.astype(BF16), vw)
        o = o / den
        o_ref[:, j * LANES:(j + 1) * LANES] = jnp.where(lane < half, o[:tq], o[tq:]).astype(BF16)


def _gqa(sink, l, q, kc, vc, k=None, v=None):
    B, Tq, _ = q.shape
    C = kc.shape[1]
    band = k is not None
    tq = min(256, Tq)
    full = lambda n: pl.BlockSpec((None, n, _GKW), lambda b, i: (b, 0, 0))
    in_specs = [pl.BlockSpec(memory_space=pltpu.SMEM),
                pl.BlockSpec((None, tq, _GQW), lambda b, i: (b, i, 0)),
                full(C), full(C)]
    args = [sink, q, kc, vc]
    seq = 0
    if band:
        seq = k.shape[1]
        assert seq >= tq + 2 * WINDOW
        in_specs += [full(seq), full(seq)]
        args += [k, v]
    return pl.pallas_call(
        functools.partial(_gqa_kernel, tq=tq, band=band, seq=seq, layer=l),
        grid=(B, Tq // tq),
        in_specs=in_specs,
        out_specs=pl.BlockSpec((None, tq, _GQW), lambda b, i: (b, i, 0)),
        out_shape=jax.ShapeDtypeStruct((B, Tq, _GQW), BF16),
        compiler_params=_params(("arbitrary", "arbitrary")),
        name="gqa",
    )(*args)


def _gelu(y):
    return 0.5 * y * (1.0 + jnp.tanh(math.sqrt(2.0 / math.pi) * (y + 0.044715 * (y * y * y))))


def _ssm_kernel(zc_ref, zl_ref, we_ref, wy_ref, a_ref, yc_ref, yl_ref, e_ref, sf_ref, sr_ref,
                *, n_ctx, n_lat, batch):
    sw = PAIR * SSM_STATE
    n_all = n_ctx + n_lat
    segs = ((zc_ref, yc_ref, 0, n_ctx), (zl_ref, yl_ref, n_ctx, n_lat))

    def chunk_rows(c0, n, pp, b):
        return pl.ds(c0 * SUBLANES + pp * batch + b, n, stride=SUBLANES)

    def z_pair(z_ref, b, pp):
        return jnp.concatenate([z_ref[b, PAIR * pp], z_ref[b, PAIR * pp + 1]], axis=1)

    for z_ref, _, c0, n in segs:
        for pp in range(PAIR):
            for b in range(batch):
                e = _dot(z_pair(z_ref, b, pp), we_ref[pp])
                for k in range(4):
                    e_ref[k, chunk_rows(c0, n, pp, b), :] = e[:, k * sw:(k + 1) * sw]
    af_re, af_im, ar_re, ar_im = a_ref[0], a_ref[1], a_ref[2], a_ref[3]
    zero = jnp.zeros((SUBLANES, sw), F32)

    def rows_of(c):
        return pl.ds(pl.multiple_of(c * SUBLANES, SUBLANES), SUBLANES)

    def fwd(c, carry):
        s_re, s_im = carry
        r = rows_of(c)
        sf_ref[0, r, :] = s_re
        sf_ref[1, r, :] = s_im
        return (af_re * s_re - af_im * s_im + e_ref[0, r, :],
                af_re * s_im + af_im * s_re + e_ref[1, r, :])

    lax.fori_loop(0, n_all, fwd, (zero, zero))

    def rev_at(c, carry):
        s_re, s_im = carry
        r = rows_of(c)
        sr_ref[0, r, :] = s_re
        sr_ref[1, r, :] = s_im
        return (ar_re * s_re - ar_im * s_im + e_ref[2, r, :],
                ar_re * s_im + ar_im * s_re + e_ref[3, r, :])

    carry = lax.fori_loop(0, n_ctx, lambda i, c: rev_at(n_ctx - 1 - i, c), (zero, zero))
    lax.fori_loop(0, n_lat, lambda i, c: rev_at(n_all - 1 - i, c), carry)

    uw = PAIR * _ZW
    for z_ref, y_ref, c0, n in segs:
        for pp in range(PAIR):
            for b in range(batch):
                r = chunk_rows(c0, n, pp, b)
                s_f = jnp.concatenate([sf_ref[0, r, :], sf_ref[1, r, :]], axis=1).astype(BF16)
                s_r = jnp.concatenate([sr_ref[0, r, :], sr_ref[1, r, :]], axis=1).astype(BF16)
                y = (_dot(z_pair(z_ref, b, pp), wy_ref[pp, 0:uw, :])
                     + _dot(s_f, wy_ref[pp, uw:uw + 2 * sw, :])
                     + _dot(s_r, wy_ref[pp, uw + 2 * sw:uw + 4 * sw, :]))
                y = _gelu(y).astype(BF16)
                y_ref[b, PAIR * pp] = y[:, :_ZW]
                y_ref[b, PAIR * pp + 1] = y[:, _ZW:]


def _ssm(z_lat, z_ctx, we, wy, a, l):
    B, G, n_lat, _ = z_lat.shape
    n_ctx = z_ctx.shape[2]
    assert PAIR * B == SUBLANES, "scan rows pack (pair, batch) into one sublane tile"
    gstep = PAIR * PAIR
    rows = (n_ctx + n_lat) * SUBLANES
    sw2 = 2 * PAIR * SSM_STATE
    uw = PAIR * _ZW
    zspec = lambda n: pl.BlockSpec((B, gstep, n, _ZW), lambda g: (0, g, 0, 0))
    y_ctx, y_lat = pl.pallas_call(
        functools.partial(_ssm_kernel, n_ctx=n_ctx, n_lat=n_lat, batch=B),
        grid=(G // gstep,),
        in_specs=[zspec(n_ctx), zspec(n_lat),
                  pl.BlockSpec((None, PAIR, uw, 2 * sw2), lambda g: (l, g, 0, 0)),
                  pl.BlockSpec((None, PAIR, uw + 2 * sw2, uw), lambda g: (l, g, 0, 0)),
                  pl.BlockSpec((None, None, 4, SUBLANES, PAIR * SSM_STATE), lambda g: (l, g, 0, 0, 0))],
        out_specs=[zspec(n_ctx), zspec(n_lat)],
        out_shape=[jax.ShapeDtypeStruct(z_ctx.shape, BF16), jax.ShapeDtypeStruct(z_lat.shape, BF16)],
        scratch_shapes=[pltpu.VMEM((4, rows, PAIR * SSM_STATE), F32),
                        pltpu.VMEM((2, rows, PAIR * SSM_STATE), F32),
                        pltpu.VMEM((2, rows, PAIR * SSM_STATE), F32)],
        compiler_params=_params(("arbitrary",)),
        name="ssm",
    )(z_ctx, z_lat, we, wy, a)
    return y_lat, y_ctx


def _merge_kernel(h_ref, mod_ref, a_ref, yg_ref, g_ref, gate_ref, wo_ref, wglu_ref, wgo_ref, wout_ref, o_ref,
                  ys_ref, *, d_model):
    D = d_model
    b0 = _dot(a_ref[...], wo_ref[...])
    nr = ys_ref.shape[1] // CHUNK
    masks = _piece_masks(nr)
    for lt in range(SSM_WIDTH // LANES):
        for jh in range(_ZW // LANES):
            a = [yg_ref[lt * _PIECES + s, :, jh * LANES:(jh + 1) * LANES].astype(F32) for s in range(_PIECES)]
            b = _block_transpose(a, masks)
            for d in range(_PIECES):
                ys_ref[lt, pl.ds(jh * _PIECES + d, nr, stride=CHUNK), :] = b[d]
    ys = jnp.concatenate([ys_ref[lt] for lt in range(SSM_WIDTH // LANES)], axis=1)
    t = _dot(ys.astype(BF16), wglu_ref[...])
    b1 = t[:, :D] * _sigmoid(t[:, D:])
    b2 = _dot(g_ref[...], wgo_ref[...])
    gate = gate_ref[...].astype(F32)
    mix = gate[:, 0:D] * b0 + gate[:, D:2 * D] * b1 + gate[:, 2 * D:3 * D] * b2
    o_ref[...] = h_ref[...] + mod_ref[5:6, :] * _dot(mix.astype(BF16), wout_ref[...])


def _merge(h, mods, l, mrow, attn, yssm, gqa, gate, lw):
    B, R, D = h.shape
    tm = min(512, R)
    row = lambda w: pl.BlockSpec((None, tm, w), lambda b, i: (b, i, 0))
    return pl.pallas_call(
        functools.partial(_merge_kernel, d_model=D),
        grid=(B, R // tm),
        in_specs=[row(D),
                  _mod_spec(mods, l, mrow),
                  row(_VW),
                  pl.BlockSpec((None, SSM_GROUPS, tm // CHUNK, _ZW), lambda b, i: (b, 0, i, 0)),
                  row(_GQW), row(N_BRANCH * D),
                  _layer(lw["wo"], l), _layer(lw["wglu"], l), _layer(lw["wgo"], l), _layer(lw["wout"], l)],
        out_specs=row(D),
        out_shape=jax.ShapeDtypeStruct((B, R, D), F32),
        scratch_shapes=[pltpu.VMEM((SSM_WIDTH // LANES, tm, LANES), F32)],
        compiler_params=_params(("arbitrary", "arbitrary")),
        name="merge",
    )(h, mods, attn, yssm, gqa, gate, lw["wo"], lw["wglu"], lw["wgo"], lw["wout"])


def _rot_index(width, n_axial):
    h = n_axial // 2
    src = np.arange(width)
    sign = np.ones(width, np.float32)
    for base in range(0, width, n_axial):
        for i in range(h):
            src[base + i], sign[base + i] = base + i + h, -1.0
            src[base + i + h], sign[base + i + h] = base + i, 1.0
    return src, sign


def _rot_cols(w, n_axial):
    src, sign = _rot_index(w.shape[-1], n_axial)
    return w[..., src] * sign


def _rope_angles(T, n):
    t = np.arange(T)
    inv = ROPE_BASE ** (-np.arange(0, n, 2, dtype=np.float32) / n)
    out = []
    for pos in (t // GRID_W, t % GRID_W):
        ang = jnp.asarray(pos.astype(np.float32))[:, None] * jnp.asarray(inv)[None, :]
        out.append((jnp.cos(ang), jnp.sin(ang)))
    cos = jnp.concatenate([out[0][0], out[0][0], out[1][0], out[1][0]], axis=1)
    sin = jnp.concatenate([out[0][1], out[0][1], out[1][1], out[1][1]], axis=1)
    return cos, sin


def _tables(T, C):
    sc_m = (MLA_NOPE + MLA_ROPE) ** -0.5 * LOG2E
    sc_g = GQA_HEAD_DIM ** -0.5 * LOG2E
    cm, sm = _rope_angles(T, MLA_ROPE // 2)
    cg, sg = _rope_angles(T, GQA_HEAD_DIM // 2)
    one = lambda n, w: jnp.ones((n, w), F32)
    zero = lambda n, w: jnp.zeros((n, w), F32)
    pad = LANES - MLA_NOPE - MLA_ROPE

    def mla_tab(n, c, s):
        cos_q = jnp.concatenate([one(n, MLA_NOPE), c, zero(n, pad)], axis=1) * sc_m
        sin_q = jnp.concatenate([zero(n, MLA_NOPE), s, zero(n, pad)], axis=1) * sc_m
        kr = jnp.concatenate([c, s, zero(n, LANES - 2 * MLA_ROPE)], axis=1)
        return jnp.concatenate([cos_q, sin_q, kr, zero(n, LANES)], axis=1)

    def gqa_tab(n, c, s):
        c2, s2 = jnp.concatenate([c, c], axis=1), jnp.concatenate([s, s], axis=1)
        return jnp.concatenate([c2 * sc_g, s2 * sc_g, c2, s2], axis=1)

    lat = (mla_tab(T, cm, sm), gqa_tab(T, cg, sg))
    ctx = (mla_tab(C, one(C, MLA_ROPE), zero(C, MLA_ROPE)),
           gqa_tab(C, one(C, GQA_HEAD_DIM), zero(C, GQA_HEAD_DIM)))
    return lat, ctx


def _place_matrix():
    e = np.zeros((LANES, _QW), np.float32)
    for h in range(MLA_HEADS):
        for j in range(MLA_ROPE):
            e[j, h * HEAD_BLK + MLA_NOPE + j] = 1.0
            e[MLA_ROPE + j, h * HEAD_BLK + MLA_NOPE + j] = 1.0
    return jnp.asarray(e, BF16)


def _layer_weights(w_in, q_norm, kv_norm, w_uq, w_ukv, w_o, w_glu, gqa_w_o, w_out):
    D = w_in.shape[0]
    offs = np.cumsum([MLA_Q_RANK, MLA_KV_RANK, MLA_ROPE, SSM_WIDTH, _GQW, _GKW, _GKW]).tolist()
    cq, ckv, kr, u, gq, gk, gv, gates = jnp.split(w_in, offs, axis=1)
    order = np.asarray(GQA_ORDER)
    gq_h = gq.reshape(D, GQA_HEADS, GQA_HEAD_DIM)
    gq_r = _rot_cols(gq_h, GQA_HEAD_DIM // 2)
    gk_r = _rot_cols(gk.reshape(D, GQA_KV_HEADS, GQA_HEAD_DIM), GQA_HEAD_DIM // 2).reshape(D, _GKW)
    kr_blk = jnp.concatenate([kr, _rot_cols(kr, MLA_ROPE // 2), jnp.zeros((D, LANES - 2 * MLA_ROPE), F32)], axis=1)
    w1 = jnp.concatenate([cq, ckv, u, gq_h[:, order].reshape(D, _GQW), gq_r[:, order].reshape(D, _GQW),
                          gk, gk_r, gv, gates, kr_blk], axis=1).astype(BF16)
    uq = w_uq.reshape(MLA_Q_RANK, MLA_HEADS, MLA_NOPE + MLA_ROPE)
    nope, rope = uq[..., :MLA_NOPE], uq[..., MLA_NOPE:]
    zpad = jnp.zeros((MLA_Q_RANK, MLA_HEADS, HEAD_BLK - MLA_NOPE - MLA_ROPE), F32)
    wq_a = jnp.concatenate([nope, rope, zpad], axis=-1).reshape(MLA_Q_RANK, _QW)
    wq_b = jnp.concatenate([jnp.zeros_like(nope), _rot_cols(rope, MLA_ROPE // 2), zpad], axis=-1)
    wq = jnp.concatenate([wq_a, wq_b.reshape(MLA_Q_RANK, _QW)], axis=1).astype(BF16)
    ukv = w_ukv.reshape(MLA_KV_RANK, MLA_HEADS, MLA_NOPE + MLA_V)
    wk = jnp.concatenate([ukv[..., :MLA_NOPE], jnp.zeros((MLA_KV_RANK, MLA_HEADS, HEAD_BLK - MLA_NOPE), F32)],
                         axis=-1).reshape(MLA_KV_RANK, _QW)
    wv = ukv[..., MLA_NOPE:].reshape(MLA_KV_RANK, _VW)
    wkv = jnp.concatenate([wk, wv], axis=1).astype(BF16)
    wgo = gqa_w_o.reshape(GQA_HEADS, GQA_HEAD_DIM, D)[order].reshape(_GQW, D).astype(BF16)
    return dict(w1=w1, qn=q_norm.reshape(1, -1), kvn=kv_norm.reshape(1, -1), wq=wq, wkv=wkv,
                wo=w_o.astype(BF16), wglu=w_glu.astype(BF16), wgo=wgo, wout=w_out.astype(BF16))


def _ssm_weights(lam_re, lam_im, log_dt, b_re, b_im, c_re, c_im, d_skip):
    hp = lax.Precision.HIGHEST
    G, P, M, Lc = SSM_GROUPS, SSM_STATE, SSM_GROUP, CHUNK
    dt = jnp.exp(log_dt)[..., None]
    kk = jnp.arange(Lc + 1, dtype=F32)[:, None, None, None]
    mag = jnp.exp(lam_re[None] * dt[None] * kk)
    pw_re, pw_im = mag * jnp.cos(lam_im[None] * dt[None] * kk), mag * jnp.sin(lam_im[None] * dt[None] * kk)
    a_re, a_im = pw_re[1], pw_im[1]
    den = lam_re * lam_re + lam_im * lam_im
    w_re = ((a_re - 1) * lam_re + a_im * lam_im) / den
    w_im = (a_im * lam_re - (a_re - 1) * lam_im) / den
    bb_re = w_re[..., None] * b_re - w_im[..., None] * b_im
    bb_im = w_re[..., None] * b_im + w_im[..., None] * b_re
    ca_re = c_re[:, None] * jnp.moveaxis(pw_re, 0, 1)[:, :, :, None, :] - c_im[:, None] * jnp.moveaxis(pw_im, 0, 1)[:, :, :, None, :]
    ca_im = c_re[:, None] * jnp.moveaxis(pw_im, 0, 1)[:, :, :, None, :] + c_im[:, None] * jnp.moveaxis(pw_re, 0, 1)[:, :, :, None, :]
    cb = (jnp.einsum('dkgop,dgpm->dkgom', ca_re, bb_re, precision=hp)
          - jnp.einsum('dkgop,dgpm->dkgom', ca_im, bb_im, precision=hp))
    lag = np.arange(Lc)[None, :] - np.arange(Lc)[:, None]
    kf = jnp.where((lag >= 0)[None, :, :, None, None], cb[0][np.clip(lag, 0, Lc)].transpose(2, 0, 1, 3, 4), 0.0)
    kr = jnp.where((lag <= 0)[None, :, :, None, None], cb[1][np.clip(-lag, 0, Lc)].transpose(2, 0, 1, 3, 4), 0.0)
    kmat = kf + kr
    skip = d_skip.reshape(G, M)
    eye = jnp.eye(Lc, dtype=F32)[None, :, :, None, None] * (jnp.eye(M, dtype=F32)[None, None, None] * skip[:, None, None, :, None])
    kmat = jnp.transpose(kmat + eye, (0, 1, 4, 2, 3)).reshape(G, Lc * M, Lc * M)
    idx_f = np.arange(Lc)[::-1].copy()
    idx_r = np.arange(Lc)

    def drive(d, idx):
        p_re, p_im = pw_re[idx, d], pw_im[idx, d]
        e_re = p_re[..., None] * bb_re[d][None] - p_im[..., None] * bb_im[d][None]
        e_im = p_re[..., None] * bb_im[d][None] + p_im[..., None] * bb_re[d][None]
        to = lambda t: jnp.transpose(t, (1, 0, 3, 2)).reshape(G, Lc * M, P)
        return to(e_re), to(e_im)

    ef_re, ef_im = drive(0, idx_f)
    er_re, er_im = drive(1, idx_r)

    def readout(d, idx):
        q_re, q_im = ca_re[d][idx], ca_im[d][idx]
        to = lambda t: jnp.transpose(t, (1, 3, 0, 2)).reshape(G, P, Lc * M)
        return to(q_re), to(-q_im)

    qf_re, qf_im = readout(0, np.arange(1, Lc + 1))
    qr_re, qr_im = readout(1, Lc - np.arange(Lc))
    npair = G // PAIR

    def pair_diag(t):
        R, Cn = t.shape[1:]
        t = t.reshape(npair, PAIR, R, Cn)
        out = jnp.zeros((npair, PAIR, R, PAIR, Cn), F32)
        for p in range(PAIR):
            out = out.at[:, p, :, p, :].set(t[:, p])
        return out.reshape(npair, PAIR * R, PAIR * Cn)

    we = jnp.concatenate([pair_diag(ef_re), pair_diag(ef_im), pair_diag(er_re), pair_diag(er_im)], axis=2)
    wy = jnp.concatenate([pair_diag(kmat), pair_diag(qf_re), pair_diag(qf_im), pair_diag(qr_re), pair_diag(qr_im)], axis=1)
    sw = PAIR * P
    nstep = npair // PAIR

    def per_row(t):
        t = t.reshape(nstep, PAIR, 1, sw)
        return jnp.broadcast_to(t, (nstep, PAIR, SUBLANES // PAIR, sw)).reshape(nstep, SUBLANES, sw)

    a_step = jnp.stack([per_row(pw_re[Lc, 0]), per_row(pw_im[Lc, 0]), per_row(pw_re[Lc, 1]), per_row(pw_im[Lc, 1])],
                       axis=1)
    return we.astype(BF16), wy.astype(BF16), a_step


def kernel(x, c, ctx, c_ctx, ada_w, ada_b, norm_ffn1, norm_mix, norm_ffn2, ffn1_w13, ffn1_w2, ffn2_w13, ffn2_w2, w_in, mla_q_norm, mla_kv_norm, mla_w_uq, mla_w_ukv, mla_w_o, ssm_lambda_re, ssm_lambda_im, ssm_log_dt, ssm_b_re, ssm_b_im, ssm_c_re, ssm_c_im, ssm_d, ssm_w_glu, gqa_sink, gqa_w_o, w_out, final_norm):
    B, T, D = x.shape
    C = ctx.shape[1]
    depth = ada_w.shape[0]
    cc = jnp.concatenate([c, c_ctx[None], jnp.zeros((SUBLANES - B - 1, D), F32)], axis=0)
    mods = _ada(cc, ada_w, ada_b).reshape(depth, SUBLANES, N_MOD, D)
    (tabm, tabg), (tabm_c, tabg_c) = _tables(T, C)
    sinks = gqa_sink * LOG2E
    lw = jax.vmap(_layer_weights)(w_in, mla_q_norm, mla_kv_norm, mla_w_uq, mla_w_ukv, mla_w_o, ssm_w_glu, gqa_w_o,
                                  w_out)
    we, wy, a_step = jax.vmap(_ssm_weights)(ssm_lambda_re, ssm_lambda_im, ssm_log_dt, ssm_b_re, ssm_b_im,
                                            ssm_c_re, ssm_c_im, ssm_d)
    e_mat = _place_matrix()
    ffn1 = (norm_ffn1[:, None, :], ffn1_w13.astype(BF16), ffn1_w2.astype(BF16))
    ffn2 = (norm_ffn2[:, None, :], ffn2_w13.astype(BF16), ffn2_w2.astype(BF16))
    g_mix = norm_mix[:, None, :]
    lat, cx = None, B
    h, hc = x, ctx
    for l in range(depth):
        ctx_out = l < depth - 1
        last = l == depth - 1
        h = _ffn(h, mods, l, lat, *ffn1, 0)
        hc = _ffn(hc, mods, l, cx, *ffn1, 0)
        q, k, v, z, gq, gk, gv, gate = _inproj(h, mods, l, lat, g_mix, lw, e_mat, tabm, tabg)
        q_c, k_c, v_c, z_c, gq_c, gk_c, gv_c, gate_c = _inproj(hc, mods, l, cx, g_mix, lw, e_mat, tabm_c, tabg_c)
        attn = _mla(q, k_c, v_c, k, v)
        y_lat, y_ctx = _ssm(z, z_c, we, wy, a_step, l)
        gqa = _gqa(sinks, l, gq, gk_c, gv_c, gk, gv)
        h = _merge(h, mods, l, lat, attn, y_lat, gqa, gate, lw)
        h = _ffn(h, mods, l, lat, *ffn2, 6, final_norm if last else None)
        if ctx_out:
            attn_c = _mla(q_c, k_c, v_c)
            gqa_c = _gqa(sinks, l, gq_c, gk_c, gv_c)
            hc = _merge(hc, mods, l, cx, attn_c, y_ctx, gqa_c, gate_c, lw)
            hc = _ffn(hc, mods, l, cx, *ffn2, 6)
    return h
```

```python
import functools
import math

import numpy as np
import jax
import jax.numpy as jnp
from jax import lax
from jax.experimental import pallas as pl
from jax.experimental.pallas import tpu as pltpu

F32 = jnp.float32
BF16 = jnp.bfloat16

GRID_W = 64
MLA_HEADS = 8
MLA_NOPE = 64
MLA_ROPE = 32
MLA_V = 64
MLA_Q_RANK = 384
MLA_KV_RANK = 256
SSM_WIDTH = 512
SSM_GROUP = 16
SSM_GROUPS = SSM_WIDTH // SSM_GROUP
SSM_STATE = 64
GQA_HEADS = 8
GQA_KV_HEADS = 2
GQA_HEAD_DIM = 64
WINDOW = 128
N_BRANCH = 3
N_MOD = 9
ROPE_BASE = 10000.0
EPS = 1e-6
NEG_INF = -1e30
LOG2E = math.log2(math.e)

LANES = 128
SUBLANES = 8
VMEM_LIMIT = 56 * 1024 * 1024
CHUNK = 16
PAIR = 2
HEAD_BLK = LANES
MLA_TQ = 1024
MLA_TK = 2048
GQA_ORDER = (0, 4, 1, 5, 2, 6, 3, 7)


def _dot(a, b):
    return jnp.dot(a, b, preferred_element_type=F32)


def _dot_nt(a, b):
    return lax.dot_general(a, b, (((1,), (1,)), ((), ())), preferred_element_type=F32)


def _rms(x, g):
    return x * lax.rsqrt(jnp.mean(x * x, axis=-1, keepdims=True) + EPS) * g


def _sigmoid(x):
    return 1.0 / (1.0 + jnp.exp(-x))


def _params(sem, vmem=VMEM_LIMIT):
    return pltpu.CompilerParams(dimension_semantics=sem, vmem_limit_bytes=vmem)


def _resident(shape):
    nd = len(shape)
    return pl.BlockSpec(shape, lambda *_: (0,) * nd, pipeline_mode=pl.Buffered(1))


def _layer(arr, l):
    nd = arr.ndim
    return pl.BlockSpec((None,) + arr.shape[1:], lambda *_: (l,) + (0,) * (nd - 1), pipeline_mode=pl.Buffered(1))


def _mod_spec(mods, l, mrow):
    blk = (None, None) + mods.shape[2:]
    if mrow is None:
        return pl.BlockSpec(blk, lambda b, i: (l, b, 0, 0))
    return pl.BlockSpec(blk, lambda b, i: (l, mrow, 0, 0))


def _ada_kernel(c_ref, w_ref, b_ref, o_ref):
    c = c_ref[...]
    s = c * _sigmoid(c)
    w = w_ref[...]
    s_hi = s.astype(BF16)
    s_lo = (s - s_hi.astype(F32)).astype(BF16)
    w_hi = w.astype(BF16)
    w_lo = (w - w_hi.astype(F32)).astype(BF16)
    o_ref[...] = _dot(s_hi, w_hi) + _dot(s_hi, w_lo) + _dot(s_lo, w_hi) + b_ref[...]


def _ada(cc, ada_w, ada_b):
    L, D, N = ada_w.shape
    tn = 1152 if N % 1152 == 0 else N
    R = cc.shape[0]
    return pl.pallas_call(
        _ada_kernel,
        grid=(L, N // tn),
        in_specs=[pl.BlockSpec((R, D), lambda l, j: (0, 0)),
                  pl.BlockSpec((None, D, tn), lambda l, j: (l, 0, j)),
                  pl.BlockSpec((None, 1, tn), lambda l, j: (l, 0, j))],
        out_specs=pl.BlockSpec((None, R, tn), lambda l, j: (l, 0, j)),
        out_shape=jax.ShapeDtypeStruct((L, R, N), F32),
        compiler_params=_params(("arbitrary", "arbitrary")),
        name="ada",
    )(cc, ada_w, ada_b.reshape(L, 1, N))


def _ffn_kernel(h_ref, mod_ref, g_ref, w13_ref, w2_ref, *rest, k0, ff, tf, final):
    if final:
        fg_ref, o_ref = rest
    else:
        (o_ref,) = rest
    x = h_ref[...]
    mod = mod_ref[...]
    xn = _rms(x, g_ref[...]) * (1.0 + mod[k0 + 1:k0 + 2]) + mod[k0:k0 + 1]
    xb = xn.astype(BF16)
    acc = None
    for f0 in range(0, ff, tf):
        a = _dot(xb, w13_ref[:, f0:f0 + tf])
        b = _dot(xb, w13_ref[:, ff + f0:ff + f0 + tf])
        act = (a * _sigmoid(a) * b).astype(BF16)
        part = _dot(act, w2_ref[f0:f0 + tf, :])
        acc = part if acc is None else acc + part
    out = x + (0.5 * mod[k0 + 2:k0 + 3]) * acc
    if final:
        out = _rms(out, fg_ref[...])
    o_ref[...] = out


def _ffn(h, mods, l, mrow, g, w13, w2, k0, final_g=None):
    B, R, D = h.shape
    ff = w2.shape[1]
    tm = min(512, R)
    tf = ff
    final = final_g is not None
    in_specs = [pl.BlockSpec((None, tm, D), lambda b, i: (b, i, 0)),
                _mod_spec(mods, l, mrow),
                _layer(g, l), _layer(w13, l), _layer(w2, l)]
    args = [h, mods, g, w13, w2]
    if final:
        in_specs.append(_resident((1, D)))
        args.append(final_g.reshape(1, D))
    return pl.pallas_call(
        functools.partial(_ffn_kernel, k0=k0, ff=ff, tf=tf, final=final),
        grid=(B, R // tm),
        in_specs=in_specs,
        out_specs=pl.BlockSpec((None, tm, D), lambda b, i: (b, i, 0)),
        out_shape=jax.ShapeDtypeStruct((B, R, D), F32),
        compiler_params=_params(("arbitrary", "arbitrary")),
        name="ffn",
    )(*args)


_O_CQ = 0
_O_GV = _O_CQ + MLA_Q_RANK
_O_CKV = _O_GV + GQA_KV_HEADS * GQA_HEAD_DIM
_O_U = _O_CKV + MLA_KV_RANK
_O_GQ = _O_U + SSM_WIDTH
_O_GQR = _O_GQ + GQA_HEADS * GQA_HEAD_DIM
_O_GK = _O_GQR + GQA_HEADS * GQA_HEAD_DIM
_O_GKR = _O_GK + GQA_KV_HEADS * GQA_HEAD_DIM
_O_GATE = _O_GKR + GQA_KV_HEADS * GQA_HEAD_DIM
_O_KR = _O_GATE + N_BRANCH * 1024
_W1_COLS = _O_KR + LANES
_QW = MLA_HEADS * HEAD_BLK
_VW = MLA_HEADS * MLA_V
_GQW = GQA_HEADS * GQA_HEAD_DIM
_GKW = GQA_KV_HEADS * GQA_HEAD_DIM


_PIECES = LANES // SSM_GROUP
_ZW = CHUNK * SSM_GROUP


def _piece_masks(rows):
    lane = lax.broadcasted_iota(jnp.int32, (rows, LANES), 1)
    return [(lane >= i * SSM_GROUP) & (lane < (i + 1) * SSM_GROUP) for i in range(_PIECES)]


def _block_transpose(a, masks):
    out = []
    for d in range(_PIECES):
        acc = None
        for s in range(_PIECES):
            shift = ((s - d) % _PIECES) * SSM_GROUP
            r = pltpu.roll(a[s], shift, 1) if shift else a[s]
            acc = r if acc is None else jnp.where(masks[s], r, acc)
        out.append(acc)
    return out


def _inproj_kernel(h_ref, mod_ref, g_ref, w1_ref, qn_ref, kvn_ref, wq_ref, wkv_ref, e_ref, tabm_ref, tabg_ref,
                   q_ref, k_ref, v_ref, z_ref, gq_ref, gk_ref, gv_ref, gate_ref, us_ref, *, d_model):
    x = h_ref[...]
    mod = mod_ref[...]
    xb = (_rms(x, g_ref[...]) * (1.0 + mod[4:5]) + mod[3:4]).astype(BF16)

    def proj(o, n):
        return _dot(xb, w1_ref[:, o:o + n])

    cq_gv = proj(_O_CQ, MLA_Q_RANK + _GKW)
    gv_ref[...] = cq_gv[:, MLA_Q_RANK:].astype(BF16)
    cqn = _rms(cq_gv[:, :MLA_Q_RANK], qn_ref[...]).astype(BF16)
    qq = _dot(cqn, wq_ref[...])
    cos_q = tabm_ref[:, 0:LANES]
    sin_q = tabm_ref[:, LANES:2 * LANES]
    for h in range(MLA_HEADS):
        a = qq[:, h * HEAD_BLK:(h + 1) * HEAD_BLK]
        b = qq[:, _QW + h * HEAD_BLK:_QW + (h + 1) * HEAD_BLK]
        q_ref[:, h * HEAD_BLK:(h + 1) * HEAD_BLK] = (a * cos_q + b * sin_q).astype(BF16)
    ckvn = _rms(proj(_O_CKV, MLA_KV_RANK), kvn_ref[...]).astype(BF16)
    kr = (proj(_O_KR, LANES) * tabm_ref[:, 2 * LANES:3 * LANES]).astype(BF16)
    k_ref[...] = (_dot(ckvn, wkv_ref[:, 0:_QW]) + _dot(kr, e_ref[...])).astype(BF16)
    v_ref[...] = _dot(ckvn, wkv_ref[:, _QW:_QW + _VW]).astype(BF16)
    u = proj(_O_U, SSM_WIDTH)
    nr = us_ref.shape[1] // CHUNK
    masks = _piece_masks(nr)
    for lt in range(SSM_WIDTH // LANES):
        us_ref[lt] = u[:, lt * LANES:(lt + 1) * LANES]
        for jh in range(_ZW // LANES):
            a = [us_ref[lt, pl.ds(jh * _PIECES + s, nr, stride=CHUNK), :] for s in range(_PIECES)]
            b = _block_transpose(a, masks)
            for d in range(_PIECES):
                z_ref[lt * _PIECES + d, :, jh * LANES:(jh + 1) * LANES] = b[d].astype(BF16)
    cos_gq = tabg_ref[:, 0:LANES]
    sin_gq = tabg_ref[:, LANES:2 * LANES]
    cos_gk = tabg_ref[:, 2 * LANES:3 * LANES]
    sin_gk = tabg_ref[:, 3 * LANES:4 * LANES]
    gq = proj(_O_GQ, _GQW)
    gqr = proj(_O_GQR, _GQW)
    for j in range(_GQW // LANES):
        sl = slice(j * LANES, (j + 1) * LANES)
        gq_ref[:, sl] = (gq[:, sl] * cos_gq + gqr[:, sl] * sin_gq).astype(BF16)
    gk2 = proj(_O_GK, 2 * _GKW)
    gk_ref[...] = (gk2[:, :_GKW] * cos_gk + gk2[:, _GKW:] * sin_gk).astype(BF16)
    gate_ref[...] = _sigmoid(proj(_O_GATE, N_BRANCH * d_model)).astype(BF16)


def _inproj(h, mods, l, mrow, g, lw, e_mat, tabm, tabg):
    B, R, D = h.shape
    tm = min(512, R)
    widths = (_QW, _QW, _VW, None, _GQW, _GKW, _GKW, N_BRANCH * D)
    row = lambda w: pl.BlockSpec((None, tm, w), lambda b, i: (b, i, 0))
    zspec = pl.BlockSpec((None, SSM_GROUPS, tm // CHUNK, _ZW), lambda b, i: (b, 0, i, 0))
    zshape = jax.ShapeDtypeStruct((B, SSM_GROUPS, R // CHUNK, _ZW), BF16)
    return pl.pallas_call(
        functools.partial(_inproj_kernel, d_model=D),
        grid=(B, R // tm),
        in_specs=[row(D),
                  _mod_spec(mods, l, mrow),
                  _layer(g, l), _layer(lw["w1"], l), _layer(lw["qn"], l), _layer(lw["kvn"], l),
                  _layer(lw["wq"], l), _layer(lw["wkv"], l),
                  _resident(e_mat.shape),
                  pl.BlockSpec((tm, 4 * LANES), lambda b, i: (i, 0)),
                  pl.BlockSpec((tm, 4 * LANES), lambda b, i: (i, 0))],
        out_specs=[zspec if w is None else row(w) for w in widths],
        out_shape=[zshape if w is None else jax.ShapeDtypeStruct((B, R, w), BF16) for w in widths],
        scratch_shapes=[pltpu.VMEM((SSM_WIDTH // LANES, tm, LANES), F32)],
        compiler_params=_params(("arbitrary", "arbitrary")),
        name="inproj",
    )(h, mods, g, lw["w1"], lw["qn"], lw["kvn"], lw["wq"], lw["wkv"], e_mat, tabm, tabg)


def _mla_kernel(q_ref, kc_ref, vc_ref, *rest, tk, n_chunks):
    if n_chunks:
        k_ref, v_ref, o_ref = rest
    else:
        (o_ref,) = rest
    tq = q_ref.shape[0]

    def tile(kb, vb, carry):
        lane_v = lax.broadcasted_iota(jnp.int32, vb.shape, 1)
        one = jnp.ones_like(vb)
        vbs = (jnp.where(lane_v < MLA_V, vb, one), jnp.where(lane_v >= MLA_V, vb, one))
        out = []
        for hh in range(2):
            m, acc = carry[hh]
            hs = slice(hh * HEAD_BLK, (hh + 1) * HEAD_BLK)
            s = _dot_nt(q_ref[:, hs], kb[:, hs])
            m_new = jnp.maximum(m, jnp.max(s, axis=-1, keepdims=True))
            alpha = jnp.exp2(m - m_new)
            p = jnp.exp2(s - m_new)
            acc = alpha * acc + _dot(p.astype(BF16), vbs[hh])
            out.append((m_new, acc))
        return tuple(out)

    init = tuple((jnp.full((tq, 1), NEG_INF, F32), jnp.zeros((tq, LANES), F32)) for _ in range(2))
    carry = tile(kc_ref[...], vc_ref[...], init)
    if n_chunks:
        def body(j, c):
            r = pl.ds(pl.multiple_of(j * tk, tk), tk)
            return tile(k_ref[r, :], v_ref[r, :], c)
        carry = lax.fori_loop(0, n_chunks, body, carry)
    lane = lax.broadcasted_iota(jnp.int32, (tq, LANES), 1)
    a0, a1 = carry[0][1], carry[1][1]
    o_ref[...] = jnp.where(lane < MLA_V, a0 / pltpu.roll(a0, MLA_V, 1), a1 / pltpu.roll(a1, MLA_V, 1)).astype(BF16)


def _mla(q, kc, vc, k=None, v=None):
    B, Tq, _ = q.shape
    C = kc.shape[1]
    tq = min(MLA_TQ, Tq)
    nh2 = MLA_HEADS // 2
    in_specs = [pl.BlockSpec((None, tq, 2 * HEAD_BLK), lambda b, h, i: (b, i, h)),
                pl.BlockSpec((None, C, 2 * HEAD_BLK), lambda b, h, i: (b, 0, h)),
                pl.BlockSpec((None, C, 2 * MLA_V), lambda b, h, i: (b, 0, h))]
    args = [q, kc, vc]
    n_chunks, tk = 0, 0
    if k is not None:
        T = k.shape[1]
        tk = min(MLA_TK, T)
        n_chunks = T // tk
        in_specs += [pl.BlockSpec((None, T, 2 * HEAD_BLK), lambda b, h, i: (b, 0, h)),
                     pl.BlockSpec((None, T, 2 * MLA_V), lambda b, h, i: (b, 0, h))]
        args += [k, v]
    return pl.pallas_call(
        functools.partial(_mla_kernel, tk=tk, n_chunks=n_chunks),
        grid=(B, nh2, Tq // tq),
        in_specs=in_specs,
        out_specs=pl.BlockSpec((None, tq, 2 * MLA_V), lambda b, h, i: (b, i, h)),
        out_shape=jax.ShapeDtypeStruct((B, Tq, _VW), BF16),
        compiler_params=_params(("arbitrary", "arbitrary", "arbitrary")),
        name="mla",
    )(*args)


def _gqa_kernel(sink_ref, q_ref, kc_ref, vc_ref, *rest, tq, band, seq, layer):
    if band:
        k_ref, v_ref, o_ref = rest
    else:
        (o_ref,) = rest
    half = GQA_HEAD_DIM
    lane = lax.broadcasted_iota(jnp.int32, (tq, LANES), 1)
    kc = kc_ref[...]
    vc = vc_ref[...]
    if band:
        nk = tq + 2 * WINDOW
        q0 = pl.program_id(1) * tq
        start = pl.multiple_of(jnp.clip(q0 - WINDOW, 0, seq - nk), LANES)
        kw = k_ref[pl.ds(start, nk), :]
        vw = v_ref[pl.ds(start, nk), :]
        r = lax.broadcasted_iota(jnp.int32, (2 * tq, nk), 0)
        qpos = q0 + jnp.where(r < tq, r, r - tq)
        kpos = start + lax.broadcasted_iota(jnp.int32, (2 * tq, nk), 1)
        valid = jnp.abs(qpos - kpos) <= WINDOW
    rows = lax.broadcasted_iota(jnp.int32, (2 * tq, 1), 0)

    def with_ones(v):
        lane_v = lax.broadcasted_iota(jnp.int32, v.shape, 1)
        one = jnp.ones_like(v)
        return jnp.where(lane_v < half, v, one), jnp.where(lane_v >= half, v, one)

    vc_lo, vc_hi = with_ones(vc)
    if band:
        vw_lo, vw_hi = with_ones(vw)
    for j in range(GQA_HEADS // 2):
        qb = q_ref[:, j * LANES:(j + 1) * LANES]
        zero = jnp.zeros_like(qb)
        qq = jnp.concatenate([jnp.where(lane < half, qb, zero), jnp.where(lane >= half, qb, zero)], axis=0)
        sink = jnp.where(rows < tq, sink_ref[layer, j], sink_ref[layer, GQA_HEADS // 2 + j])
        s_c = _dot_nt(qq, kc)
        m = jnp.maximum(sink, jnp.max(s_c, axis=-1, keepdims=True))
        if band:
            s_b = jnp.where(valid, _dot_nt(qq, kw), NEG_INF)
            m = jnp.maximum(m, jnp.max(s_b, axis=-1, keepdims=True))
        p_c = jnp.exp2(s_c - m).astype(BF16)
        o_lo = _dot(p_c[:tq], vc_lo)
        o_hi = _dot(p_c[tq:], vc_hi)
        if band:
            p_b = jnp.exp2(s_b - m).astype(BF16)
            o_lo = o_lo + _dot(p_b[:tq], vw_lo)
            o_hi = o_hi + _dot(p_b[tq:], vw_hi)
        e_sink = jnp.exp2(sink - m)
        den_lo = pltpu.roll(o_lo, half, 1) + e_sink[:tq]
        den_hi = pltpu.roll(o_hi, half, 1) + e_sink[tq:]
        o_ref[:, j * LANES:(j + 1) * LANES] = jnp.where(lane < half, o_lo / den_lo, o_hi / den_hi).astype(BF16)


def _gqa(sink, l, q, kc, vc, k=None, v=None):
    B, Tq, _ = q.shape
    C = kc.shape[1]
    band = k is not None
    tq = min(256, Tq)
    full = lambda n: pl.BlockSpec((None, n, _GKW), lambda b, i: (b, 0, 0))
    in_specs = [pl.BlockSpec(memory_space=pltpu.SMEM),
                pl.BlockSpec((None, tq, _GQW), lambda b, i: (b, i, 0)),
                full(C), full(C)]
    args = [sink, q, kc, vc]
    seq = 0
    if band:
        seq = k.shape[1]
        assert seq >= tq + 2 * WINDOW
        in_specs += [full(seq), full(seq)]
        args += [k, v]
    return pl.pallas_call(
        functools.partial(_gqa_kernel, tq=tq, band=band, seq=seq, layer=l),
        grid=(B, Tq // tq),
        in_specs=in_specs,
        out_specs=pl.BlockSpec((None, tq, _GQW), lambda b, i: (b, i, 0)),
        out_shape=jax.ShapeDtypeStruct((B, Tq, _GQW), BF16),
        compiler_params=_params(("arbitrary", "arbitrary")),
        name="gqa",
    )(*args)


def _gelu(y):
    return 0.5 * y * (1.0 + jnp.tanh(math.sqrt(2.0 / math.pi) * (y + 0.044715 * (y * y * y))))


def _ssm_kernel(zc_ref, zl_ref, we_ref, wy_ref, a_ref, yc_ref, yl_ref, e_ref, sf_ref, sr_ref,
                *, n_ctx, n_lat, batch):
    sw = PAIR * SSM_STATE
    n_all = n_ctx + n_lat
    segs = ((zc_ref, yc_ref, 0, n_ctx), (zl_ref, yl_ref, n_ctx, n_lat))

    def chunk_rows(c0, n, pp, b):
        return pl.ds(c0 * SUBLANES + pp * batch + b, n, stride=SUBLANES)

    def z_pair(z_ref, b, pp):
        return jnp.concatenate([z_ref[b, PAIR * pp], z_ref[b, PAIR * pp + 1]], axis=1)

    for z_ref, _, c0, n in segs:
        for pp in range(PAIR):
            for b in range(batch):
                e = _dot(z_pair(z_ref, b, pp), we_ref[pp])
                for k in range(4):
                    e_ref[k, chunk_rows(c0, n, pp, b), :] = e[:, k * sw:(k + 1) * sw]
    af_re, af_im, ar_re, ar_im = a_ref[0], a_ref[1], a_ref[2], a_ref[3]
    zero = jnp.zeros((SUBLANES, sw), F32)

    def rows_of(c):
        return pl.ds(pl.multiple_of(c * SUBLANES, SUBLANES), SUBLANES)

    def fwd(c, carry):
        s_re, s_im = carry
        r = rows_of(c)
        sf_ref[0, r, :] = s_re
        sf_ref[1, r, :] = s_im
        return (af_re * s_re - af_im * s_im + e_ref[0, r, :],
                af_re * s_im + af_im * s_re + e_ref[1, r, :])

    lax.fori_loop(0, n_all, fwd, (zero, zero))

    def rev_at(c, carry):
        s_re, s_im = carry
        r = rows_of(c)
        sr_ref[0, r, :] = s_re
        sr_ref[1, r, :] = s_im
        return (ar_re * s_re - ar_im * s_im + e_ref[2, r, :],
                ar_re * s_im + ar_im * s_re + e_ref[3, r, :])

    carry = lax.fori_loop(0, n_ctx, lambda i, c: rev_at(n_ctx - 1 - i, c), (zero, zero))
    lax.fori_loop(0, n_lat, lambda i, c: rev_at(n_all - 1 - i, c), carry)

    uw = PAIR * _ZW
    for z_ref, y_ref, c0, n in segs:
        for pp in range(PAIR):
            for b in range(batch):
                r = chunk_rows(c0, n, pp, b)
                s_f = jnp.concatenate([sf_ref[0, r, :], sf_ref[1, r, :]], axis=1).astype(BF16)
                s_r = jnp.concatenate([sr_ref[0, r, :], sr_ref[1, r, :]], axis=1).astype(BF16)
                y = (_dot(z_pair(z_ref, b, pp), wy_ref[pp, 0:uw, :])
                     + _dot(s_f, wy_ref[pp, uw:uw + 2 * sw, :])
                     + _dot(s_r, wy_ref[pp, uw + 2 * sw:uw + 4 * sw, :]))
                y = _gelu(y).astype(BF16)
                y_ref[b, PAIR * pp] = y[:, :_ZW]
                y_ref[b, PAIR * pp + 1] = y[:, _ZW:]


def _ssm(z_lat, z_ctx, we, wy, a, l):
    B, G, n_lat, _ = z_lat.shape
    n_ctx = z_ctx.shape[2]
    assert PAIR * B == SUBLANES, "scan rows pack (pair, batch) into one sublane tile"
    gstep = PAIR * PAIR
    rows = (n_ctx + n_lat) * SUBLANES
    sw2 = 2 * PAIR * SSM_STATE
    uw = PAIR * _ZW
    zspec = lambda n: pl.BlockSpec((B, gstep, n, _ZW), lambda g: (0, g, 0, 0))
    y_ctx, y_lat = pl.pallas_call(
        functools.partial(_ssm_kernel, n_ctx=n_ctx, n_lat=n_lat, batch=B),
        grid=(G // gstep,),
        in_specs=[zspec(n_ctx), zspec(n_lat),
                  pl.BlockSpec((None, PAIR, uw, 2 * sw2), lambda g: (l, g, 0, 0)),
                  pl.BlockSpec((None, PAIR, uw + 2 * sw2, uw), lambda g: (l, g, 0, 0)),
                  pl.BlockSpec((None, None, 4, SUBLANES, PAIR * SSM_STATE), lambda g: (l, g, 0, 0, 0))],
        out_specs=[zspec(n_ctx), zspec(n_lat)],
        out_shape=[jax.ShapeDtypeStruct(z_ctx.shape, BF16), jax.ShapeDtypeStruct(z_lat.shape, BF16)],
        scratch_shapes=[pltpu.VMEM((4, rows, PAIR * SSM_STATE), F32),
                        pltpu.VMEM((2, rows, PAIR * SSM_STATE), F32),
                        pltpu.VMEM((2, rows, PAIR * SSM_STATE), F32)],
        compiler_params=_params(("arbitrary",)),
        name="ssm",
    )(z_ctx, z_lat, we, wy, a)
    return y_lat, y_ctx


def _merge_kernel(h_ref, mod_ref, a_ref, yg_ref, g_ref, gate_ref, wo_ref, wglu_ref, wgo_ref, wout_ref, o_ref,
                  ys_ref, *, d_model):
    D = d_model
    b0 = _dot(a_ref[...], wo_ref[...])
    nr = ys_ref.shape[1] // CHUNK
    masks = _piece_masks(nr)
    for lt in range(SSM_WIDTH // LANES):
        for jh in range(_ZW // LANES):
            a = [yg_ref[lt * _PIECES + s, :, jh * LANES:(jh + 1) * LANES].astype(F32) for s in range(_PIECES)]
            b = _block_transpose(a, masks)
            for d in range(_PIECES):
                ys_ref[lt, pl.ds(jh * _PIECES + d, nr, stride=CHUNK), :] = b[d]
    ys = jnp.concatenate([ys_ref[lt] for lt in range(SSM_WIDTH // LANES)], axis=1)
    t = _dot(ys.astype(BF16), wglu_ref[...])
    b1 = t[:, :D] * _sigmoid(t[:, D:])
    b2 = _dot(g_ref[...], wgo_ref[...])
    gate = gate_ref[...].astype(F32)
    mix = gate[:, 0:D] * b0 + gate[:, D:2 * D] * b1 + gate[:, 2 * D:3 * D] * b2
    o_ref[...] = h_ref[...] + mod_ref[5:6, :] * _dot(mix.astype(BF16), wout_ref[...])


def _merge(h, mods, l, mrow, attn, yssm, gqa, gate, lw):
    B, R, D = h.shape
    tm = min(512, R)
    row = lambda w: pl.BlockSpec((None, tm, w), lambda b, i: (b, i, 0))
    return pl.pallas_call(
        functools.partial(_merge_kernel, d_model=D),
        grid=(B, R // tm),
        in_specs=[row(D),
                  _mod_spec(mods, l, mrow),
                  row(_VW),
                  pl.BlockSpec((None, SSM_GROUPS, tm // CHUNK, _ZW), lambda b, i: (b, 0, i, 0)),
                  row(_GQW), row(N_BRANCH * D),
                  _layer(lw["wo"], l), _layer(lw["wglu"], l), _layer(lw["wgo"], l), _layer(lw["wout"], l)],
        out_specs=row(D),
        out_shape=jax.ShapeDtypeStruct((B, R, D), F32),
        scratch_shapes=[pltpu.VMEM((SSM_WIDTH // LANES, tm, LANES), F32)],
        compiler_params=_params(("arbitrary", "arbitrary")),
        name="merge",
    )(h, mods, attn, yssm, gqa, gate, lw["wo"], lw["wglu"], lw["wgo"], lw["wout"])


def _rot_index(width, n_axial):
    h = n_axial // 2
    src = np.arange(width)
    sign = np.ones(width, np.float32)
    for base in range(0, width, n_axial):
        for i in range(h):
            src[base + i], sign[base + i] = base + i + h, -1.0
            src[base + i + h], sign[base + i + h] = base + i, 1.0
    return src, sign


def _rot_cols(w, n_axial):
    src, sign = _rot_index(w.shape[-1], n_axial)
    return w[..., src] * sign


def _rope_angles(T, n):
    t = np.arange(T)
    inv = ROPE_BASE ** (-np.arange(0, n, 2, dtype=np.float32) / n)
    out = []
    for pos in (t // GRID_W, t % GRID_W):
        ang = jnp.asarray(pos.astype(np.float32))[:, None] * jnp.asarray(inv)[None, :]
        out.append((jnp.cos(ang), jnp.sin(ang)))
    cos = jnp.concatenate([out[0][0], out[0][0], out[1][0], out[1][0]], axis=1)
    sin = jnp.concatenate([out[0][1], out[0][1], out[1][1], out[1][1]], axis=1)
    return cos, sin


def _tables(T, C):
    sc_m = (MLA_NOPE + MLA_ROPE) ** -0.5 * LOG2E
    sc_g = GQA_HEAD_DIM ** -0.5 * LOG2E
    cm, sm = _rope_angles(T, MLA_ROPE // 2)
    cg, sg = _rope_angles(T, GQA_HEAD_DIM // 2)
    one = lambda n, w: jnp.ones((n, w), F32)
    zero = lambda n, w: jnp.zeros((n, w), F32)
    pad = LANES - MLA_NOPE - MLA_ROPE

    def mla_tab(n, c, s):
        cos_q = jnp.concatenate([one(n, MLA_NOPE), c, zero(n, pad)], axis=1) * sc_m
        sin_q = jnp.concatenate([zero(n, MLA_NOPE), s, zero(n, pad)], axis=1) * sc_m
        kr = jnp.concatenate([c, s, zero(n, LANES - 2 * MLA_ROPE)], axis=1)
        return jnp.concatenate([cos_q, sin_q, kr, zero(n, LANES)], axis=1)

    def gqa_tab(n, c, s):
        c2, s2 = jnp.concatenate([c, c], axis=1), jnp.concatenate([s, s], axis=1)
        return jnp.concatenate([c2 * sc_g, s2 * sc_g, c2, s2], axis=1)

    lat = (mla_tab(T, cm, sm), gqa_tab(T, cg, sg))
    ctx = (mla_tab(C, one(C, MLA_ROPE), zero(C, MLA_ROPE)),
           gqa_tab(C, one(C, GQA_HEAD_DIM), zero(C, GQA_HEAD_DIM)))
    return lat, ctx


def _place_matrix():
    e = np.zeros((LANES, _QW), np.float32)
    for h in range(MLA_HEADS):
        for j in range(MLA_ROPE):
            e[j, h * HEAD_BLK + MLA_NOPE + j] = 1.0
            e[MLA_ROPE + j, h * HEAD_BLK + MLA_NOPE + j] = 1.0
    return jnp.asarray(e, BF16)


def _layer_weights(w_in, q_norm, kv_norm, w_uq, w_ukv, w_o, w_glu, gqa_w_o, w_out):
    D = w_in.shape[0]
    offs = np.cumsum([MLA_Q_RANK, MLA_KV_RANK, MLA_ROPE, SSM_WIDTH, _GQW, _GKW, _GKW]).tolist()
    cq, ckv, kr, u, gq, gk, gv, gates = jnp.split(w_in, offs, axis=1)
    order = np.asarray(GQA_ORDER)
    gq_h = gq.reshape(D, GQA_HEADS, GQA_HEAD_DIM)
    gq_r = _rot_cols(gq_h, GQA_HEAD_DIM // 2)
    gk_r = _rot_cols(gk.reshape(D, GQA_KV_HEADS, GQA_HEAD_DIM), GQA_HEAD_DIM // 2).reshape(D, _GKW)
    kr_blk = jnp.concatenate([kr, _rot_cols(kr, MLA_ROPE // 2), jnp.zeros((D, LANES - 2 * MLA_ROPE), F32)], axis=1)
    w1 = jnp.concatenate([cq, gv, ckv, u, gq_h[:, order].reshape(D, _GQW), gq_r[:, order].reshape(D, _GQW),
                          gk, gk_r, gates, kr_blk], axis=1).astype(BF16)
    uq = w_uq.reshape(MLA_Q_RANK, MLA_HEADS, MLA_NOPE + MLA_ROPE)
    nope, rope = uq[..., :MLA_NOPE], uq[..., MLA_NOPE:]
    zpad = jnp.zeros((MLA_Q_RANK, MLA_HEADS, HEAD_BLK - MLA_NOPE - MLA_ROPE), F32)
    wq_a = jnp.concatenate([nope, rope, zpad], axis=-1).reshape(MLA_Q_RANK, _QW)
    wq_b = jnp.concatenate([jnp.zeros_like(nope), _rot_cols(rope, MLA_ROPE // 2), zpad], axis=-1)
    wq = jnp.concatenate([wq_a, wq_b.reshape(MLA_Q_RANK, _QW)], axis=1).astype(BF16)
    ukv = w_ukv.reshape(MLA_KV_RANK, MLA_HEADS, MLA_NOPE + MLA_V)
    wk = jnp.concatenate([ukv[..., :MLA_NOPE], jnp.zeros((MLA_KV_RANK, MLA_HEADS, HEAD_BLK - MLA_NOPE), F32)],
                         axis=-1).reshape(MLA_KV_RANK, _QW)
    wv = ukv[..., MLA_NOPE:].reshape(MLA_KV_RANK, _VW)
    wkv = jnp.concatenate([wk, wv], axis=1).astype(BF16)
    wgo = gqa_w_o.reshape(GQA_HEADS, GQA_HEAD_DIM, D)[order].reshape(_GQW, D).astype(BF16)
    return dict(w1=w1, qn=q_norm.reshape(1, -1), kvn=kv_norm.reshape(1, -1), wq=wq, wkv=wkv,
                wo=w_o.astype(BF16), wglu=w_glu.astype(BF16), wgo=wgo, wout=w_out.astype(BF16))


def _ssm_weights(lam_re, lam_im, log_dt, b_re, b_im, c_re, c_im, d_skip):
    G, P, M, Lc = SSM_GROUPS, SSM_STATE, SSM_GROUP, CHUNK
    dt = jnp.exp(log_dt)[..., None]
    kk = jnp.arange(Lc + 1, dtype=F32)[:, None, None, None]
    mag = jnp.exp(lam_re[None] * dt[None] * kk)
    pw_re, pw_im = mag * jnp.cos(lam_im[None] * dt[None] * kk), mag * jnp.sin(lam_im[None] * dt[None] * kk)
    a_re, a_im = pw_re[1], pw_im[1]
    den = lam_re * lam_re + lam_im * lam_im
    w_re = ((a_re - 1) * lam_re + a_im * lam_im) / den
    w_im = (a_im * lam_re - (a_re - 1) * lam_im) / den
    bb_re = w_re[..., None] * b_re - w_im[..., None] * b_im
    bb_im = w_re[..., None] * b_im + w_im[..., None] * b_re
    ca_re = c_re[:, None] * jnp.moveaxis(pw_re, 0, 1)[:, :, :, None, :] - c_im[:, None] * jnp.moveaxis(pw_im, 0, 1)[:, :, :, None, :]
    ca_im = c_re[:, None] * jnp.moveaxis(pw_im, 0, 1)[:, :, :, None, :] + c_im[:, None] * jnp.moveaxis(pw_re, 0, 1)[:, :, :, None, :]
    cb = jnp.sum(ca_re[..., None] * bb_re[:, None, :, None] - ca_im[..., None] * bb_im[:, None, :, None], axis=-2)
    lags_f = jnp.transpose(cb[0, :Lc], (1, 3, 0, 2)).reshape(G, M, Lc * M)
    lags_r = jnp.transpose(cb[1, :Lc][::-1], (1, 3, 0, 2)).reshape(G, M, Lc * M)
    halo = jnp.zeros((G, M, (Lc - 1) * M), F32)
    ext = jnp.concatenate([halo, lags_f], axis=-1) + jnp.concatenate([lags_r, halo], axis=-1)
    kmat = jnp.stack([ext[:, :, (Lc - 1 - i) * M:(Lc - 1 - i) * M + Lc * M] for i in range(Lc)], axis=1)
    skip = jnp.tile(d_skip.reshape(G, M), (1, Lc))
    kmat = kmat.reshape(G, Lc * M, Lc * M) + jnp.eye(Lc * M, dtype=F32)[None] * skip[:, None, :]
    idx_f = np.arange(Lc)[::-1].copy()
    idx_r = np.arange(Lc)

    def drive(d, idx):
        p_re, p_im = pw_re[idx, d], pw_im[idx, d]
        e_re = p_re[..., None] * bb_re[d][None] - p_im[..., None] * bb_im[d][None]
        e_im = p_re[..., None] * bb_im[d][None] + p_im[..., None] * bb_re[d][None]
        to = lambda t: jnp.transpose(t, (1, 0, 3, 2)).reshape(G, Lc * M, P)
        return to(e_re), to(e_im)

    ef_re, ef_im = drive(0, idx_f)
    er_re, er_im = drive(1, idx_r)

    def readout(d, idx):
        q_re, q_im = ca_re[d][idx], ca_im[d][idx]
        to = lambda t: jnp.transpose(t, (1, 3, 0, 2)).reshape(G, P, Lc * M)
        return to(q_re), to(-q_im)

    qf_re, qf_im = readout(0, np.arange(1, Lc + 1))
    qr_re, qr_im = readout(1, Lc - np.arange(Lc))
    npair = G // PAIR

    def pair_diag(t):
        R, Cn = t.shape[1:]
        t = t.reshape(npair, PAIR, R, Cn)
        z = jnp.zeros((npair, R, Cn), F32)
        rows = [jnp.concatenate([t[:, p] if q == p else z for q in range(PAIR)], axis=-1) for p in range(PAIR)]
        return jnp.concatenate(rows, axis=1)

    we = jnp.concatenate([pair_diag(ef_re), pair_diag(ef_im), pair_diag(er_re), pair_diag(er_im)], axis=2)
    wy = jnp.concatenate([pair_diag(kmat), pair_diag(qf_re), pair_diag(qf_im), pair_diag(qr_re), pair_diag(qr_im)], axis=1)
    sw = PAIR * P
    nstep = npair // PAIR

    def per_row(t):
        t = t.reshape(nstep, PAIR, 1, sw)
        return jnp.broadcast_to(t, (nstep, PAIR, SUBLANES // PAIR, sw)).reshape(nstep, SUBLANES, sw)

    a_step = jnp.stack([per_row(pw_re[Lc, 0]), per_row(pw_im[Lc, 0]), per_row(pw_re[Lc, 1]), per_row(pw_im[Lc, 1])],
                       axis=1)
    return we.astype(BF16), wy.astype(BF16), a_step


def kernel(x, c, ctx, c_ctx, ada_w, ada_b, norm_ffn1, norm_mix, norm_ffn2, ffn1_w13, ffn1_w2, ffn2_w13, ffn2_w2, w_in, mla_q_norm, mla_kv_norm, mla_w_uq, mla_w_ukv, mla_w_o, ssm_lambda_re, ssm_lambda_im, ssm_log_dt, ssm_b_re, ssm_b_im, ssm_c_re, ssm_c_im, ssm_d, ssm_w_glu, gqa_sink, gqa_w_o, w_out, final_norm):
    B, T, D = x.shape
    C = ctx.shape[1]
    depth = ada_w.shape[0]
    cc = jnp.concatenate([c, c_ctx[None], jnp.zeros((SUBLANES - B - 1, D), F32)], axis=0)
    mods = _ada(cc, ada_w, ada_b).reshape(depth, SUBLANES, N_MOD, D)
    (tabm, tabg), (tabm_c, tabg_c) = _tables(T, C)
    sinks = gqa_sink * LOG2E
    lw = jax.vmap(_layer_weights)(w_in, mla_q_norm, mla_kv_norm, mla_w_uq, mla_w_ukv, mla_w_o, ssm_w_glu, gqa_w_o,
                                  w_out)
    we, wy, a_step = jax.vmap(_ssm_weights)(ssm_lambda_re, ssm_lambda_im, ssm_log_dt, ssm_b_re, ssm_b_im,
                                            ssm_c_re, ssm_c_im, ssm_d)
    e_mat = _place_matrix()
    ffn1 = (norm_ffn1[:, None, :], ffn1_w13.astype(BF16), ffn1_w2.astype(BF16))
    ffn2 = (norm_ffn2[:, None, :], ffn2_w13.astype(BF16), ffn2_w2.astype(BF16))
    g_mix = norm_mix[:, None, :]
    lat, cx = None, B
    h, hc = x, ctx
    for l in range(depth):
        ctx_out = l < depth - 1
        last = l == depth - 1
        h = _ffn(h, mods, l, lat, *ffn1, 0)
        hc = _ffn(hc, mods, l, cx, *ffn1, 0)
        q, k, v, z, gq, gk, gv, gate = _inproj(h, mods, l, lat, g_mix, lw, e_mat, tabm, tabg)
        q_c, k_c, v_c, z_c, gq_c, gk_c, gv_c, gate_c = _inproj(hc, mods, l, cx, g_mix, lw, e_mat, tabm_c, tabg_c)
        attn = _mla(q, k_c, v_c, k, v)
        y_lat, y_ctx = _ssm(z, z_c, we, wy, a_step, l)
        gqa = _gqa(sinks, l, gq, gk_c, gv_c, gk, gv)
        h = _merge(h, mods, l, lat, attn, y_lat, gqa, gate, lw)
        h = _ffn(h, mods, l, lat, *ffn2, 6, final_norm if last else None)
        if ctx_out:
            attn_c = _mla(q_c, k_c, v_c)
            gqa_c = _gqa(sinks, l, gq_c, gk_c, gv_c)
            hc = _merge(hc, mods, l, cx, attn_c, y_ctx, gqa_c, gate_c, lw)
            hc = _ffn(hc, mods, l, cx, *ffn2, 6)
    return h
```

```python
import functools
import math

import numpy as np
import jax
import jax.numpy as jnp
from jax import lax
from jax.experimental import pallas as pl
from jax.experimental.pallas import tpu as pltpu

F32 = jnp.float32
BF16 = jnp.bfloat16

GRID_W = 64
MLA_HEADS = 8
MLA_NOPE = 64
MLA_ROPE = 32
MLA_V = 64
MLA_Q_RANK = 384
MLA_KV_RANK = 256
SSM_WIDTH = 512
SSM_GROUP = 16
SSM_GROUPS = SSM_WIDTH // SSM_GROUP
SSM_STATE = 64
GQA_HEADS = 8
GQA_KV_HEADS = 2
GQA_HEAD_DIM = 64
WINDOW = 128
N_BRANCH = 3
N_MOD = 9
ROPE_BASE = 10000.0
EPS = 1e-6
NEG_INF = -1e30
LOG2E = math.log2(math.e)

LANES = 128
SUBLANES = 8
VMEM_LIMIT = 56 * 1024 * 1024
CHUNK = 16
PAIR = 2
HEAD_BLK = LANES
MLA_TQ = 1024
MLA_TK = 2048
GQA_ORDER = (0, 4, 1, 5, 2, 6, 3, 7)


def _dot(a, b):
    return jnp.dot(a, b, preferred_element_type=F32)


def _dot_nt(a, b):
    return lax.dot_general(a, b, (((1,), (1,)), ((), ())), preferred_element_type=F32)


def _rms(x, g):
    return x * lax.rsqrt(jnp.mean(x * x, axis=-1, keepdims=True) + EPS) * g


def _sigmoid(x):
    return 1.0 / (1.0 + jnp.exp(-x))


def _params(sem, vmem=VMEM_LIMIT):
    return pltpu.CompilerParams(dimension_semantics=sem, vmem_limit_bytes=vmem)


def _resident(shape):
    nd = len(shape)
    return pl.BlockSpec(shape, lambda *_: (0,) * nd, pipeline_mode=pl.Buffered(1))


def _layer(arr, l):
    nd = arr.ndim
    return pl.BlockSpec((None,) + arr.shape[1:], lambda *_: (l,) + (0,) * (nd - 1), pipeline_mode=pl.Buffered(1))


def _mod_spec(mods, l, mrow):
    blk = (None, None) + mods.shape[2:]
    if mrow is None:
        return pl.BlockSpec(blk, lambda b, i: (l, b, 0, 0))
    return pl.BlockSpec(blk, lambda b, i: (l, mrow, 0, 0))


def _ada_kernel(c_ref, w_ref, b_ref, o_ref):
    c = c_ref[...]
    s = c * _sigmoid(c)
    w = w_ref[...]
    s_hi = s.astype(BF16)
    s_lo = (s - s_hi.astype(F32)).astype(BF16)
    w_hi = w.astype(BF16)
    w_lo = (w - w_hi.astype(F32)).astype(BF16)
    o_ref[...] = _dot(s_hi, w_hi) + _dot(s_hi, w_lo) + _dot(s_lo, w_hi) + b_ref[...]


def _ada(cc, ada_w, ada_b):
    L, D, N = ada_w.shape
    tn = 1152 if N % 1152 == 0 else N
    R = cc.shape[0]
    return pl.pallas_call(
        _ada_kernel,
        grid=(L, N // tn),
        in_specs=[pl.BlockSpec((R, D), lambda l, j: (0, 0)),
                  pl.BlockSpec((None, D, tn), lambda l, j: (l, 0, j)),
                  pl.BlockSpec((None, 1, tn), lambda l, j: (l, 0, j))],
        out_specs=pl.BlockSpec((None, R, tn), lambda l, j: (l, 0, j)),
        out_shape=jax.ShapeDtypeStruct((L, R, N), F32),
        compiler_params=_params(("arbitrary", "arbitrary")),
        name="ada",
    )(cc, ada_w, ada_b.reshape(L, 1, N))


def _ffn_kernel(h_ref, mod_ref, g_ref, w13_ref, w2_ref, *rest, k0, ff, tf, final):
    if final:
        fg_ref, o_ref = rest
    else:
        (o_ref,) = rest
    x = h_ref[...]
    mod = mod_ref[...]
    xn = _rms(x, g_ref[...]) * (1.0 + mod[k0 + 1:k0 + 2]) + mod[k0:k0 + 1]
    xb = xn.astype(BF16)
    acc = None
    for f0 in range(0, ff, tf):
        a = _dot(xb, w13_ref[:, f0:f0 + tf])
        b = _dot(xb, w13_ref[:, ff + f0:ff + f0 + tf])
        act = (a * _sigmoid(a) * b).astype(BF16)
        part = _dot(act, w2_ref[f0:f0 + tf, :])
        acc = part if acc is None else acc + part
    out = x + (0.5 * mod[k0 + 2:k0 + 3]) * acc
    if final:
        out = _rms(out, fg_ref[...])
    o_ref[...] = out


def _ffn(h, mods, l, mrow, g, w13, w2, k0, final_g=None):
    B, R, D = h.shape
    ff = w2.shape[1]
    tm = min(512, R)
    tf = ff
    final = final_g is not None
    in_specs = [pl.BlockSpec((None, tm, D), lambda b, i: (b, i, 0)),
                _mod_spec(mods, l, mrow),
                _layer(g, l), _layer(w13, l), _layer(w2, l)]
    args = [h, mods, g, w13, w2]
    if final:
        in_specs.append(_resident((1, D)))
        args.append(final_g.reshape(1, D))
    return pl.pallas_call(
        functools.partial(_ffn_kernel, k0=k0, ff=ff, tf=tf, final=final),
        grid=(B, R // tm),
        in_specs=in_specs,
        out_specs=pl.BlockSpec((None, tm, D), lambda b, i: (b, i, 0)),
        out_shape=jax.ShapeDtypeStruct((B, R, D), F32),
        compiler_params=_params(("arbitrary", "arbitrary")),
        name="ffn",
    )(*args)


_O_CQ = 0
_O_GV = _O_CQ + MLA_Q_RANK
_O_CKV = _O_GV + GQA_KV_HEADS * GQA_HEAD_DIM
_O_U = _O_CKV + MLA_KV_RANK
_O_GQ = _O_U + SSM_WIDTH
_O_GK = _O_GQ + GQA_HEADS * GQA_HEAD_DIM
_O_KR = _O_GK + GQA_KV_HEADS * GQA_HEAD_DIM
_O_GATE = _O_KR + LANES
_QW = MLA_HEADS * HEAD_BLK
_VW = MLA_HEADS * MLA_V
_GQW = GQA_HEADS * GQA_HEAD_DIM
_GKW = GQA_KV_HEADS * GQA_HEAD_DIM


_PIECES = LANES // SSM_GROUP
_ZW = CHUNK * SSM_GROUP


def _piece_masks(rows):
    lane = lax.broadcasted_iota(jnp.int32, (rows, LANES), 1)
    return [(lane >= i * SSM_GROUP) & (lane < (i + 1) * SSM_GROUP) for i in range(_PIECES)]


def _block_transpose(a, masks):
    out = []
    for d in range(_PIECES):
        acc = None
        for s in range(_PIECES):
            shift = ((s - d) % _PIECES) * SSM_GROUP
            r = pltpu.roll(a[s], shift, 1) if shift else a[s]
            acc = r if acc is None else jnp.where(masks[s], r, acc)
        out.append(acc)
    return out


def _inproj_kernel(h_ref, mod_ref, g_ref, w1_ref, qn_ref, kvn_ref, wq_ref, wkv_ref, e_ref, tabm_ref, tabg_ref,
                   q_ref, k_ref, v_ref, z_ref, gq_ref, gk_ref, gv_ref, gate_ref, us_ref, *, d_model):
    x = h_ref[...]
    mod = mod_ref[...]
    xb = (_rms(x, g_ref[...]) * (1.0 + mod[4:5]) + mod[3:4]).astype(BF16)

    def proj(o, n):
        return _dot(xb, w1_ref[:, o:o + n])

    lane = lax.broadcasted_iota(jnp.int32, (x.shape[0], LANES), 1)

    def rope(t, n_half, cos, sin_signed):
        first = (lane & (2 * n_half - 1)) < n_half
        partner = jnp.where(first, pltpu.roll(t, LANES - n_half, 1), pltpu.roll(t, n_half, 1))
        return t * cos + partner * sin_signed

    cq_gv = proj(_O_CQ, MLA_Q_RANK + _GKW)
    gv_ref[...] = cq_gv[:, MLA_Q_RANK:].astype(BF16)
    cqn = _rms(cq_gv[:, :MLA_Q_RANK], qn_ref[...]).astype(BF16)
    qq = _dot(cqn, wq_ref[...])
    cos_q = tabm_ref[:, 0:LANES]
    sin_q = tabm_ref[:, LANES:2 * LANES]
    for h in range(MLA_HEADS):
        hs = slice(h * HEAD_BLK, (h + 1) * HEAD_BLK)
        q_ref[:, hs] = rope(qq[:, hs], MLA_ROPE // 4, cos_q, sin_q).astype(BF16)
    ckvn = _rms(proj(_O_CKV, MLA_KV_RANK), kvn_ref[...]).astype(BF16)
    gk_kr = proj(_O_GK, _GKW + LANES)
    kr = rope(gk_kr[:, _GKW:], MLA_ROPE // 4, tabm_ref[:, 2 * LANES:3 * LANES], tabm_ref[:, 3 * LANES:4 * LANES])
    k_ref[...] = (_dot(ckvn, wkv_ref[:, 0:_QW]) + _dot(kr.astype(BF16), e_ref[...])).astype(BF16)
    v_ref[...] = _dot(ckvn, wkv_ref[:, _QW:_QW + _VW]).astype(BF16)
    u = proj(_O_U, SSM_WIDTH)
    nr = us_ref.shape[1] // CHUNK
    masks = _piece_masks(nr)
    for lt in range(SSM_WIDTH // LANES):
        us_ref[lt] = u[:, lt * LANES:(lt + 1) * LANES]
        for jh in range(_ZW // LANES):
            a = [us_ref[lt, pl.ds(jh * _PIECES + s, nr, stride=CHUNK), :] for s in range(_PIECES)]
            b = _block_transpose(a, masks)
            for d in range(_PIECES):
                z_ref[lt * _PIECES + d, :, jh * LANES:(jh + 1) * LANES] = b[d].astype(BF16)
    cos_gq = tabg_ref[:, 0:LANES]
    sin_gq = tabg_ref[:, LANES:2 * LANES]
    cos_gk = tabg_ref[:, 2 * LANES:3 * LANES]
    sin_gk = tabg_ref[:, 3 * LANES:4 * LANES]
    gq = proj(_O_GQ, _GQW)
    for j in range(_GQW // LANES):
        sl = slice(j * LANES, (j + 1) * LANES)
        gq_ref[:, sl] = rope(gq[:, sl], GQA_HEAD_DIM // 4, cos_gq, sin_gq).astype(BF16)
    gk_ref[...] = rope(gk_kr[:, :_GKW], GQA_HEAD_DIM // 4, cos_gk, sin_gk).astype(BF16)
    gate_ref[...] = _sigmoid(proj(_O_GATE, N_BRANCH * d_model)).astype(BF16)


def _inproj(h, mods, l, mrow, g, lw, e_mat, tabm, tabg):
    B, R, D = h.shape
    tm = min(512, R)
    widths = (_QW, _QW, _VW, None, _GQW, _GKW, _GKW, N_BRANCH * D)
    row = lambda w: pl.BlockSpec((None, tm, w), lambda b, i: (b, i, 0))
    zspec = pl.BlockSpec((None, SSM_GROUPS, tm // CHUNK, _ZW), lambda b, i: (b, 0, i, 0))
    zshape = jax.ShapeDtypeStruct((B, SSM_GROUPS, R // CHUNK, _ZW), BF16)
    return pl.pallas_call(
        functools.partial(_inproj_kernel, d_model=D),
        grid=(B, R // tm),
        in_specs=[row(D),
                  _mod_spec(mods, l, mrow),
                  _layer(g, l), _layer(lw["w1"], l), _layer(lw["qn"], l), _layer(lw["kvn"], l),
                  _layer(lw["wq"], l), _layer(lw["wkv"], l),
                  _resident(e_mat.shape),
                  pl.BlockSpec((tm, 4 * LANES), lambda b, i: (i, 0)),
                  pl.BlockSpec((tm, 4 * LANES), lambda b, i: (i, 0))],
        out_specs=[zspec if w is None else row(w) for w in widths],
        out_shape=[zshape if w is None else jax.ShapeDtypeStruct((B, R, w), BF16) for w in widths],
        scratch_shapes=[pltpu.VMEM((SSM_WIDTH // LANES, tm, LANES), F32)],
        compiler_params=_params(("arbitrary", "arbitrary")),
        name="inproj",
    )(h, mods, g, lw["w1"], lw["qn"], lw["kvn"], lw["wq"], lw["wkv"], e_mat, tabm, tabg)


def _mla_kernel(q_ref, kc_ref, vc_ref, *rest, tk, n_chunks):
    if n_chunks:
        k_ref, v_ref, o_ref = rest
    else:
        (o_ref,) = rest
    tq = q_ref.shape[0]

    def tile(kb, vb, carry):
        lane_v = lax.broadcasted_iota(jnp.int32, vb.shape, 1)
        one = jnp.ones_like(vb)
        vbs = (jnp.where(lane_v < MLA_V, vb, one), jnp.where(lane_v >= MLA_V, vb, one))
        out = []
        for hh in range(2):
            m, acc = carry[hh]
            hs = slice(hh * HEAD_BLK, (hh + 1) * HEAD_BLK)
            s = _dot_nt(q_ref[:, hs], kb[:, hs])
            m_new = jnp.maximum(m, jnp.max(s, axis=-1, keepdims=True))
            alpha = jnp.exp2(m - m_new)
            p = jnp.exp2(s - m_new)
            acc = alpha * acc + _dot(p.astype(BF16), vbs[hh])
            out.append((m_new, acc))
        return tuple(out)

    init = tuple((jnp.full((tq, 1), NEG_INF, F32), jnp.zeros((tq, LANES), F32)) for _ in range(2))
    carry = tile(kc_ref[...], vc_ref[...], init)
    if n_chunks:
        def body(j, c):
            r = pl.ds(pl.multiple_of(j * tk, tk), tk)
            return tile(k_ref[r, :], v_ref[r, :], c)
        carry = lax.fori_loop(0, n_chunks, body, carry, unroll=True)
    lane = lax.broadcasted_iota(jnp.int32, (tq, LANES), 1)
    a0, a1 = carry[0][1], carry[1][1]
    o_ref[...] = jnp.where(lane < MLA_V, a0 / pltpu.roll(a0, MLA_V, 1), a1 / pltpu.roll(a1, MLA_V, 1)).astype(BF16)


def _mla(q, kc, vc, k=None, v=None):
    B, Tq, _ = q.shape
    C = kc.shape[1]
    tq = min(MLA_TQ, Tq)
    nh2 = MLA_HEADS // 2
    in_specs = [pl.BlockSpec((None, tq, 2 * HEAD_BLK), lambda b, h, i: (b, i, h)),
                pl.BlockSpec((None, C, 2 * HEAD_BLK), lambda b, h, i: (b, 0, h)),
                pl.BlockSpec((None, C, 2 * MLA_V), lambda b, h, i: (b, 0, h))]
    args = [q, kc, vc]
    n_chunks, tk = 0, 0
    if k is not None:
        T = k.shape[1]
        tk = min(MLA_TK, T)
        n_chunks = T // tk
        in_specs += [pl.BlockSpec((None, T, 2 * HEAD_BLK), lambda b, h, i: (b, 0, h)),
                     pl.BlockSpec((None, T, 2 * MLA_V), lambda b, h, i: (b, 0, h))]
        args += [k, v]
    return pl.pallas_call(
        functools.partial(_mla_kernel, tk=tk, n_chunks=n_chunks),
        grid=(B, nh2, Tq // tq),
        in_specs=in_specs,
        out_specs=pl.BlockSpec((None, tq, 2 * MLA_V), lambda b, h, i: (b, i, h)),
        out_shape=jax.ShapeDtypeStruct((B, Tq, _VW), BF16),
        compiler_params=_params(("arbitrary", "arbitrary", "arbitrary")),
        name="mla",
    )(*args)


def _gqa_kernel(sink_ref, q_ref, kc_ref, vc_ref, *rest, tq, band, seq, layer):
    if band:
        k_ref, v_ref, o_ref = rest
    else:
        (o_ref,) = rest
    half = GQA_HEAD_DIM
    lane = lax.broadcasted_iota(jnp.int32, (tq, LANES), 1)
    kc = kc_ref[...]
    vc = vc_ref[...]
    if band:
        nk = tq + 2 * WINDOW
        q0 = pl.program_id(1) * tq
        start = pl.multiple_of(jnp.clip(q0 - WINDOW, 0, seq - nk), LANES)
        kw = k_ref[pl.ds(start, nk), :]
        vw = v_ref[pl.ds(start, nk), :]
        r = lax.broadcasted_iota(jnp.int32, (2 * tq, nk), 0)
        qpos = q0 + jnp.where(r < tq, r, r - tq)
        kpos = start + lax.broadcasted_iota(jnp.int32, (2 * tq, nk), 1)
        valid = jnp.abs(qpos - kpos) <= WINDOW
    rows = lax.broadcasted_iota(jnp.int32, (2 * tq, 1), 0)
    for j in range(GQA_HEADS // 2):
        qb = q_ref[:, j * LANES:(j + 1) * LANES]
        zero = jnp.zeros_like(qb)
        qq = jnp.concatenate([jnp.where(lane < half, qb, zero), jnp.where(lane >= half, qb, zero)], axis=0)
        sink = jnp.where(rows < tq, sink_ref[layer, j], sink_ref[layer, GQA_HEADS // 2 + j])
        s_c = _dot_nt(qq, kc)
        m = jnp.maximum(sink, jnp.max(s_c, axis=-1, keepdims=True))
        if band:
            s_b = jnp.where(valid, _dot_nt(qq, kw), NEG_INF)
            m = jnp.maximum(m, jnp.max(s_b, axis=-1, keepdims=True))
        p_c = jnp.exp2(s_c - m)
        den = jnp.exp2(sink - m) + jnp.sum(p_c, axis=-1, keepdims=True)
        o = _dot(p_c.astype(BF16), vc)
        if band:
            p_b = jnp.exp2(s_b - m)
            den = den + jnp.sum(p_b, axis=-1, keepdims=True)
            o = o + _dot(p_b.astype(BF16), vw)
        o = o / den
        o_ref[:, j * LANES:(j + 1) * LANES] = jnp.where(lane < half, o[:tq], o[tq:]).astype(BF16)


def _gqa(sink, l, q, kc, vc, k=None, v=None):
    B, Tq, _ = q.shape
    C = kc.shape[1]
    band = k is not None
    tq = min(256, Tq)
    full = lambda n: pl.BlockSpec((None, n, _GKW), lambda b, i: (b, 0, 0))
    in_specs = [pl.BlockSpec(memory_space=pltpu.SMEM),
                pl.BlockSpec((None, tq, _GQW), lambda b, i: (b, i, 0)),
                full(C), full(C)]
    args = [sink, q, kc, vc]
    seq = 0
    if band:
        seq = k.shape[1]
        assert seq >= tq + 2 * WINDOW
        in_specs += [full(seq), full(seq)]
        args += [k, v]
    return pl.pallas_call(
        functools.partial(_gqa_kernel, tq=tq, band=band, seq=seq, layer=l),
        grid=(B, Tq // tq),
        in_specs=in_specs,
        out_specs=pl.BlockSpec((None, tq, _GQW), lambda b, i: (b, i, 0)),
        out_shape=jax.ShapeDtypeStruct((B, Tq, _GQW), BF16),
        compiler_params=_params(("arbitrary", "arbitrary")),
        name="gqa",
    )(*args)


def _gelu(y):
    return 0.5 * y * (1.0 + jnp.tanh(math.sqrt(2.0 / math.pi) * (y + 0.044715 * (y * y * y))))


def _ssm_kernel(zc_ref, zl_ref, we_ref, wy_ref, a_ref, yc_ref, yl_ref, e_ref, sf_ref, sr_ref,
                *, n_ctx, n_lat, batch):
    sw = PAIR * SSM_STATE
    n_all = n_ctx + n_lat
    segs = ((zc_ref, yc_ref, 0, n_ctx), (zl_ref, yl_ref, n_ctx, n_lat))

    def chunk_rows(c0, n, pp, b):
        return pl.ds(c0 * SUBLANES + pp * batch + b, n, stride=SUBLANES)

    def z_pair(z_ref, b, pp):
        return jnp.concatenate([z_ref[b, PAIR * pp], z_ref[b, PAIR * pp + 1]], axis=1)

    for z_ref, _, c0, n in segs:
        for pp in range(PAIR):
            for b in range(batch):
                e = _dot(z_pair(z_ref, b, pp), we_ref[pp])
                for k in range(4):
                    e_ref[k, chunk_rows(c0, n, pp, b), :] = e[:, k * sw:(k + 1) * sw]
    af_re, af_im, ar_re, ar_im = a_ref[0], a_ref[1], a_ref[2], a_ref[3]
    zero = jnp.zeros((SUBLANES, sw), F32)

    def rows_of(c):
        return pl.ds(pl.multiple_of(c * SUBLANES, SUBLANES), SUBLANES)

    def fwd(c, carry):
        s_re, s_im = carry
        r = rows_of(c)
        sf_ref[0, r, :] = s_re
        sf_ref[1, r, :] = s_im
        return (af_re * s_re - af_im * s_im + e_ref[0, r, :],
                af_re * s_im + af_im * s_re + e_ref[1, r, :])

    lax.fori_loop(0, n_all, fwd, (zero, zero))

    def rev_at(c, carry):
        s_re, s_im = carry
        r = rows_of(c)
        sr_ref[0, r, :] = s_re
        sr_ref[1, r, :] = s_im
        return (ar_re * s_re - ar_im * s_im + e_ref[2, r, :],
                ar_re * s_im + ar_im * s_re + e_ref[3, r, :])

    carry = lax.fori_loop(0, n_ctx, lambda i, c: rev_at(n_ctx - 1 - i, c), (zero, zero))
    lax.fori_loop(0, n_lat, lambda i, c: rev_at(n_all - 1 - i, c), carry)

    uw = PAIR * _ZW
    for z_ref, y_ref, c0, n in segs:
        for pp in range(PAIR):
            for b in range(batch):
                r = chunk_rows(c0, n, pp, b)
                s_f = jnp.concatenate([sf_ref[0, r, :], sf_ref[1, r, :]], axis=1).astype(BF16)
                s_r = jnp.concatenate([sr_ref[0, r, :], sr_ref[1, r, :]], axis=1).astype(BF16)
                y = (_dot(z_pair(z_ref, b, pp), wy_ref[pp, 0:uw, :])
                     + _dot(s_f, wy_ref[pp, uw:uw + 2 * sw, :])
                     + _dot(s_r, wy_ref[pp, uw + 2 * sw:uw + 4 * sw, :]))
                y = _gelu(y).astype(BF16)
                y_ref[b, PAIR * pp] = y[:, :_ZW]
                y_ref[b, PAIR * pp + 1] = y[:, _ZW:]


def _ssm(z_lat, z_ctx, we, wy, a, l):
    B, G, n_lat, _ = z_lat.shape
    n_ctx = z_ctx.shape[2]
    assert PAIR * B == SUBLANES, "scan rows pack (pair, batch) into one sublane tile"
    gstep = PAIR * PAIR
    rows = (n_ctx + n_lat) * SUBLANES
    sw2 = 2 * PAIR * SSM_STATE
    uw = PAIR * _ZW
    zspec = lambda n: pl.BlockSpec((B, gstep, n, _ZW), lambda g: (0, g, 0, 0))
    y_ctx, y_lat = pl.pallas_call(
        functools.partial(_ssm_kernel, n_ctx=n_ctx, n_lat=n_lat, batch=B),
        grid=(G // gstep,),
        in_specs=[zspec(n_ctx), zspec(n_lat),
                  pl.BlockSpec((None, PAIR, uw, 2 * sw2), lambda g: (l, g, 0, 0)),
                  pl.BlockSpec((None, PAIR, uw + 2 * sw2, uw), lambda g: (l, g, 0, 0)),
                  pl.BlockSpec((None, None, 4, SUBLANES, PAIR * SSM_STATE), lambda g: (l, g, 0, 0, 0))],
        out_specs=[zspec(n_ctx), zspec(n_lat)],
        out_shape=[jax.ShapeDtypeStruct(z_ctx.shape, BF16), jax.ShapeDtypeStruct(z_lat.shape, BF16)],
        scratch_shapes=[pltpu.VMEM((4, rows, PAIR * SSM_STATE), F32),
                        pltpu.VMEM((2, rows, PAIR * SSM_STATE), F32),
                        pltpu.VMEM((2, rows, PAIR * SSM_STATE), F32)],
        compiler_params=_params(("arbitrary",)),
        name="ssm",
    )(z_ctx, z_lat, we, wy, a)
    return y_lat, y_ctx


def _merge_kernel(h_ref, mod_ref, a_ref, yg_ref, g_ref, gate_ref, wo_ref, wglu_ref, wgo_ref, wout_ref, o_ref,
                  ys_ref, *, d_model):
    D = d_model
    b0 = _dot(a_ref[...], wo_ref[...])
    nr = ys_ref.shape[1] // CHUNK
    masks = _piece_masks(nr)
    for lt in range(SSM_WIDTH // LANES):
        for jh in range(_ZW // LANES):
            a = [yg_ref[lt * _PIECES + s, :, jh * LANES:(jh + 1) * LANES].astype(F32) for s in range(_PIECES)]
            b = _block_transpose(a, masks)
            for d in range(_PIECES):
                ys_ref[lt, pl.ds(jh * _PIECES + d, nr, stride=CHUNK), :] = b[d]
    ys = jnp.concatenate([ys_ref[lt] for lt in range(SSM_WIDTH // LANES)], axis=1)
    t = _dot(ys.astype(BF16), wglu_ref[...])
    b1 = t[:, :D] * _sigmoid(t[:, D:])
    b2 = _dot(g_ref[...], wgo_ref[...])
    gate = gate_ref[...].astype(F32)
    mix = gate[:, 0:D] * b0 + gate[:, D:2 * D] * b1 + gate[:, 2 * D:3 * D] * b2
    o_ref[...] = h_ref[...] + mod_ref[5:6, :] * _dot(mix.astype(BF16), wout_ref[...])


def _merge(h, mods, l, mrow, attn, yssm, gqa, gate, lw):
    B, R, D = h.shape
    tm = min(512, R)
    row = lambda w: pl.BlockSpec((None, tm, w), lambda b, i: (b, i, 0))
    return pl.pallas_call(
        functools.partial(_merge_kernel, d_model=D),
        grid=(B, R // tm),
        in_specs=[row(D),
                  _mod_spec(mods, l, mrow),
                  row(_VW),
                  pl.BlockSpec((None, SSM_GROUPS, tm // CHUNK, _ZW), lambda b, i: (b, 0, i, 0)),
                  row(_GQW), row(N_BRANCH * D),
                  _layer(lw["wo"], l), _layer(lw["wglu"], l), _layer(lw["wgo"], l), _layer(lw["wout"], l)],
        out_specs=row(D),
        out_shape=jax.ShapeDtypeStruct((B, R, D), F32),
        scratch_shapes=[pltpu.VMEM((SSM_WIDTH // LANES, tm, LANES), F32)],
        compiler_params=_params(("arbitrary", "arbitrary")),
        name="merge",
    )(h, mods, attn, yssm, gqa, gate, lw["wo"], lw["wglu"], lw["wgo"], lw["wout"])


def _rope_angles(T, n):
    t = np.arange(T)
    inv = ROPE_BASE ** (-np.arange(0, n, 2, dtype=np.float32) / n)
    out = []
    for pos in (t // GRID_W, t % GRID_W):
        ang = jnp.asarray(pos.astype(np.float32))[:, None] * jnp.asarray(inv)[None, :]
        out.append((jnp.cos(ang), jnp.sin(ang)))
    cos = jnp.concatenate([out[0][0], out[0][0], out[1][0], out[1][0]], axis=1)
    sin = jnp.concatenate([-out[0][1], out[0][1], -out[1][1], out[1][1]], axis=1)
    return cos, sin


def _tables(T, C):
    sc_m = (MLA_NOPE + MLA_ROPE) ** -0.5 * LOG2E
    sc_g = GQA_HEAD_DIM ** -0.5 * LOG2E
    cm, sm = _rope_angles(T, MLA_ROPE // 2)
    cg, sg = _rope_angles(T, GQA_HEAD_DIM // 2)
    one = lambda n, w: jnp.ones((n, w), F32)
    zero = lambda n, w: jnp.zeros((n, w), F32)
    pad = LANES - MLA_NOPE - MLA_ROPE

    def mla_tab(n, c, s):
        cos_q = jnp.concatenate([one(n, MLA_NOPE), c, zero(n, pad)], axis=1) * sc_m
        sin_q = jnp.concatenate([zero(n, MLA_NOPE), s, zero(n, pad)], axis=1) * sc_m
        kr_cos = jnp.concatenate([c, zero(n, LANES - MLA_ROPE)], axis=1)
        kr_sin = jnp.concatenate([s, zero(n, LANES - MLA_ROPE)], axis=1)
        return jnp.concatenate([cos_q, sin_q, kr_cos, kr_sin], axis=1)

    def gqa_tab(n, c, s):
        c2, s2 = jnp.concatenate([c, c], axis=1), jnp.concatenate([s, s], axis=1)
        return jnp.concatenate([c2 * sc_g, s2 * sc_g, c2, s2], axis=1)

    lat = (mla_tab(T, cm, sm), gqa_tab(T, cg, sg))
    ctx = (mla_tab(C, one(C, MLA_ROPE), zero(C, MLA_ROPE)),
           gqa_tab(C, one(C, GQA_HEAD_DIM), zero(C, GQA_HEAD_DIM)))
    return lat, ctx


def _place_matrix():
    e = np.zeros((LANES, _QW), np.float32)
    for h in range(MLA_HEADS):
        for j in range(MLA_ROPE):
            e[j, h * HEAD_BLK + MLA_NOPE + j] = 1.0
    return jnp.asarray(e, BF16)


def _layer_weights(w_in, q_norm, kv_norm, w_uq, w_ukv, w_o, w_glu, gqa_w_o, w_out):
    D = w_in.shape[0]
    offs = np.cumsum([MLA_Q_RANK, MLA_KV_RANK, MLA_ROPE, SSM_WIDTH, _GQW, _GKW, _GKW]).tolist()
    cq, ckv, kr, u, gq, gk, gv, gates = jnp.split(w_in, offs, axis=1)
    order = np.asarray(GQA_ORDER)
    gq_h = gq.reshape(D, GQA_HEADS, GQA_HEAD_DIM)
    kr_blk = jnp.concatenate([kr, jnp.zeros((D, LANES - MLA_ROPE), F32)], axis=1)
    w1 = jnp.concatenate([cq, gv, ckv, u, gq_h[:, order].reshape(D, _GQW), gk, kr_blk, gates], axis=1).astype(BF16)
    uq = w_uq.reshape(MLA_Q_RANK, MLA_HEADS, MLA_NOPE + MLA_ROPE)
    zpad = jnp.zeros((MLA_Q_RANK, MLA_HEADS, HEAD_BLK - MLA_NOPE - MLA_ROPE), F32)
    wq = jnp.concatenate([uq, zpad], axis=-1).reshape(MLA_Q_RANK, _QW).astype(BF16)
    ukv = w_ukv.reshape(MLA_KV_RANK, MLA_HEADS, MLA_NOPE + MLA_V)
    wk = jnp.concatenate([ukv[..., :MLA_NOPE], jnp.zeros((MLA_KV_RANK, MLA_HEADS, HEAD_BLK - MLA_NOPE), F32)],
                         axis=-1).reshape(MLA_KV_RANK, _QW)
    wv = ukv[..., MLA_NOPE:].reshape(MLA_KV_RANK, _VW)
    wkv = jnp.concatenate([wk, wv], axis=1).astype(BF16)
    wgo = gqa_w_o.reshape(GQA_HEADS, GQA_HEAD_DIM, D)[order].reshape(_GQW, D).astype(BF16)
    return dict(w1=w1, qn=q_norm.reshape(1, -1), kvn=kv_norm.reshape(1, -1), wq=wq, wkv=wkv,
                wo=w_o.astype(BF16), wglu=w_glu.astype(BF16), wgo=wgo, wout=w_out.astype(BF16))


def _ssm_weights(lam_re, lam_im, log_dt, b_re, b_im, c_re, c_im, d_skip):
    G, P, M, Lc = SSM_GROUPS, SSM_STATE, SSM_GROUP, CHUNK
    dt = jnp.exp(log_dt)[..., None]
    kk = jnp.arange(Lc + 1, dtype=F32)[:, None, None, None]
    mag = jnp.exp(lam_re[None] * dt[None] * kk)
    pw_re, pw_im = mag * jnp.cos(lam_im[None] * dt[None] * kk), mag * jnp.sin(lam_im[None] * dt[None] * kk)
    a_re, a_im = pw_re[1], pw_im[1]
    den = lam_re * lam_re + lam_im * lam_im
    w_re = ((a_re - 1) * lam_re + a_im * lam_im) / den
    w_im = (a_im * lam_re - (a_re - 1) * lam_im) / den
    bb_re = w_re[..., None] * b_re - w_im[..., None] * b_im
    bb_im = w_re[..., None] * b_im + w_im[..., None] * b_re
    ca_re = c_re[:, None] * jnp.moveaxis(pw_re, 0, 1)[:, :, :, None, :] - c_im[:, None] * jnp.moveaxis(pw_im, 0, 1)[:, :, :, None, :]
    ca_im = c_re[:, None] * jnp.moveaxis(pw_im, 0, 1)[:, :, :, None, :] + c_im[:, None] * jnp.moveaxis(pw_re, 0, 1)[:, :, :, None, :]
    cb = jnp.sum(ca_re[..., None] * bb_re[:, None, :, None] - ca_im[..., None] * bb_im[:, None, :, None], axis=-2)
    lags_f = jnp.transpose(cb[0, :Lc], (1, 3, 0, 2)).reshape(G, M, Lc * M)
    lags_r = jnp.transpose(cb[1, :Lc][::-1], (1, 3, 0, 2)).reshape(G, M, Lc * M)
    halo = jnp.zeros((G, M, (Lc - 1) * M), F32)
    ext = jnp.concatenate([halo, lags_f], axis=-1) + jnp.concatenate([lags_r, halo], axis=-1)
    kmat = jnp.stack([ext[:, :, (Lc - 1 - i) * M:(Lc - 1 - i) * M + Lc * M] for i in range(Lc)], axis=1)
    skip = jnp.tile(d_skip.reshape(G, M), (1, Lc))
    kmat = kmat.reshape(G, Lc * M, Lc * M) + jnp.eye(Lc * M, dtype=F32)[None] * skip[:, None, :]
    idx_f = np.arange(Lc)[::-1].copy()
    idx_r = np.arange(Lc)

    def drive(d, idx):
        p_re, p_im = pw_re[idx, d], pw_im[idx, d]
        e_re = p_re[..., None] * bb_re[d][None] - p_im[..., None] * bb_im[d][None]
        e_im = p_re[..., None] * bb_im[d][None] + p_im[..., None] * bb_re[d][None]
        to = lambda t: jnp.transpose(t, (1, 0, 3, 2)).reshape(G, Lc * M, P)
        return to(e_re), to(e_im)

    ef_re, ef_im = drive(0, idx_f)
    er_re, er_im = drive(1, idx_r)

    def readout(d, idx):
        q_re, q_im = ca_re[d][idx], ca_im[d][idx]
        to = lambda t: jnp.transpose(t, (1, 3, 0, 2)).reshape(G, P, Lc * M)
        return to(q_re), to(-q_im)

    qf_re, qf_im = readout(0, np.arange(1, Lc + 1))
    qr_re, qr_im = readout(1, Lc - np.arange(Lc))
    npair = G // PAIR

    def pair_diag(t):
        R, Cn = t.shape[1:]
        t = t.reshape(npair, PAIR, R, Cn)
        z = jnp.zeros((npair, R, Cn), F32)
        rows = [jnp.concatenate([t[:, p] if q == p else z for q in range(PAIR)], axis=-1) for p in range(PAIR)]
        return jnp.concatenate(rows, axis=1)

    we = jnp.concatenate([pair_diag(ef_re), pair_diag(ef_im), pair_diag(er_re), pair_diag(er_im)], axis=2)
    wy = jnp.concatenate([pair_diag(kmat), pair_diag(qf_re), pair_diag(qf_im), pair_diag(qr_re), pair_diag(qr_im)], axis=1)
    sw = PAIR * P
    nstep = npair // PAIR

    def per_row(t):
        t = t.reshape(nstep, PAIR, 1, sw)
        return jnp.broadcast_to(t, (nstep, PAIR, SUBLANES // PAIR, sw)).reshape(nstep, SUBLANES, sw)

    a_step = jnp.stack([per_row(pw_re[Lc, 0]), per_row(pw_im[Lc, 0]), per_row(pw_re[Lc, 1]), per_row(pw_im[Lc, 1])],
                       axis=1)
    return we.astype(BF16), wy.astype(BF16), a_step


def kernel(x, c, ctx, c_ctx, ada_w, ada_b, norm_ffn1, norm_mix, norm_ffn2, ffn1_w13, ffn1_w2, ffn2_w13, ffn2_w2, w_in, mla_q_norm, mla_kv_norm, mla_w_uq, mla_w_ukv, mla_w_o, ssm_lambda_re, ssm_lambda_im, ssm_log_dt, ssm_b_re, ssm_b_im, ssm_c_re, ssm_c_im, ssm_d, ssm_w_glu, gqa_sink, gqa_w_o, w_out, final_norm):
    B, T, D = x.shape
    C = ctx.shape[1]
    depth = ada_w.shape[0]
    cc = jnp.concatenate([c, c_ctx[None], jnp.zeros((SUBLANES - B - 1, D), F32)], axis=0)
    mods = _ada(cc, ada_w, ada_b).reshape(depth, SUBLANES, N_MOD, D)
    (tabm, tabg), (tabm_c, tabg_c) = _tables(T, C)
    sinks = gqa_sink * LOG2E
    lw = jax.vmap(_layer_weights)(w_in, mla_q_norm, mla_kv_norm, mla_w_uq, mla_w_ukv, mla_w_o, ssm_w_glu, gqa_w_o,
                                  w_out)
    we, wy, a_step = jax.vmap(_ssm_weights)(ssm_lambda_re, ssm_lambda_im, ssm_log_dt, ssm_b_re, ssm_b_im,
                                            ssm_c_re, ssm_c_im, ssm_d)
    e_mat = _place_matrix()
    ffn1 = (norm_ffn1[:, None, :], ffn1_w13.astype(BF16), ffn1_w2.astype(BF16))
    ffn2 = (norm_ffn2[:, None, :], ffn2_w13.astype(BF16), ffn2_w2.astype(BF16))
    g_mix = norm_mix[:, None, :]
    lat, cx = None, B
    h, hc = x, ctx
    for l in range(depth):
        ctx_out = l < depth - 1
        last = l == depth - 1
        h = _ffn(h, mods, l, lat, *ffn1, 0)
        hc = _ffn(hc, mods, l, cx, *ffn1, 0)
        q, k, v, z, gq, gk, gv, gate = _inproj(h, mods, l, lat, g_mix, lw, e_mat, tabm, tabg)
        q_c, k_c, v_c, z_c, gq_c, gk_c, gv_c, gate_c = _inproj(hc, mods, l, cx, g_mix, lw, e_mat, tabm_c, tabg_c)
        attn = _mla(q, k_c, v_c, k, v)
        y_lat, y_ctx = _ssm(z, z_c, we, wy, a_step, l)
        gqa = _gqa(sinks, l, gq, gk_c, gv_c, gk, gv)
        h = _merge(h, mods, l, lat, attn, y_lat, gqa, gate, lw)
        h = _ffn(h, mods, l, lat, *ffn2, 6, final_norm if last else None)
        if ctx_out:
            attn_c = _mla(q_c, k_c, v_c)
            gqa_c = _gqa(sinks, l, gq_c, gk_c, gv_c)
            hc = _merge(hc, mods, l, cx, attn_c, y_ctx, gqa_c, gate_c, lw)
            hc = _ffn(hc, mods, l, cx, *ffn2, 6)
    return h
```

```python
import functools
import math

import numpy as np
import jax
import jax.numpy as jnp
from jax import lax
from jax.experimental import pallas as pl
from jax.experimental.pallas import tpu as pltpu

F32 = jnp.float32
BF16 = jnp.bfloat16

GRID_W = 64
MLA_HEADS = 8
MLA_NOPE = 64
MLA_ROPE = 32
MLA_V = 64
MLA_Q_RANK = 384
MLA_KV_RANK = 256
SSM_WIDTH = 512
SSM_GROUP = 16
SSM_GROUPS = SSM_WIDTH // SSM_GROUP
SSM_STATE = 64
GQA_HEADS = 8
GQA_KV_HEADS = 2
GQA_HEAD_DIM = 64
WINDOW = 128
N_BRANCH = 3
N_MOD = 9
ROPE_BASE = 10000.0
EPS = 1e-6
NEG_INF = -1e30
LOG2E = math.log2(math.e)

LANES = 128
SUBLANES = 8
VMEM_LIMIT = 56 * 1024 * 1024
CHUNK = 16
PAIR = 2
HEAD_BLK = LANES
MXU_COLS = 256
FFN_TM = 1024
GQA_TQ = 256
MLA_TQ = 1024
MLA_TK = 2048
GQA_ORDER = (0, 4, 1, 5, 2, 6, 3, 7)


def _dot(a, b):
    return jnp.dot(a, b, preferred_element_type=F32)


def _dot_nt(a, b):
    return lax.dot_general(a, b, (((1,), (1,)), ((), ())), preferred_element_type=F32)


def _rms(x, g):
    return x * lax.rsqrt(jnp.mean(x * x, axis=-1, keepdims=True) + EPS) * g


def _sigmoid(x):
    return 0.5 * jnp.tanh(0.5 * x) + 0.5


def _params(sem, vmem=VMEM_LIMIT):
    return pltpu.CompilerParams(dimension_semantics=sem, vmem_limit_bytes=vmem)


def _resident(shape):
    nd = len(shape)
    return pl.BlockSpec(shape, lambda *_: (0,) * nd, pipeline_mode=pl.Buffered(1))


def _layer(arr, l):
    nd = arr.ndim
    return pl.BlockSpec((None,) + arr.shape[1:], lambda *_: (l,) + (0,) * (nd - 1), pipeline_mode=pl.Buffered(1))


def _mod_spec(mods, l, mrow):
    blk = (None, None) + mods.shape[2:]
    if mrow is None:
        return pl.BlockSpec(blk, lambda b, i: (l, b, 0, 0))
    return pl.BlockSpec(blk, lambda b, i: (l, mrow, 0, 0))


def _ada_kernel(c_ref, w_ref, b_ref, o_ref):
    c = c_ref[...]
    s = c * _sigmoid(c)
    w = w_ref[...]
    s_hi = s.astype(BF16)
    s_lo = (s - s_hi.astype(F32)).astype(BF16)
    w_hi = w.astype(BF16)
    w_lo = (w - w_hi.astype(F32)).astype(BF16)
    o_ref[...] = _dot(s_hi, w_hi) + _dot(s_hi, w_lo) + _dot(s_lo, w_hi) + b_ref[...]


def _ada(cc, ada_w, ada_b):
    L, D, N = ada_w.shape
    tn = 1152 if N % 1152 == 0 else N
    R = cc.shape[0]
    return pl.pallas_call(
        _ada_kernel,
        grid=(L, N // tn),
        in_specs=[pl.BlockSpec((R, D), lambda l, j: (0, 0)),
                  pl.BlockSpec((None, D, tn), lambda l, j: (l, 0, j)),
                  pl.BlockSpec((None, 1, tn), lambda l, j: (l, 0, j))],
        out_specs=pl.BlockSpec((None, R, tn), lambda l, j: (l, 0, j)),
        out_shape=jax.ShapeDtypeStruct((L, R, N), F32),
        compiler_params=_params(("arbitrary", "arbitrary")),
        name="ada",
    )(cc, ada_w, ada_b.reshape(L, 1, N))


def _ffn_kernel(h_ref, mod_ref, g_ref, w13_ref, w2_ref, *rest, k0, ff, tf, final):
    if final:
        fg_ref, o_ref = rest
    else:
        (o_ref,) = rest
    x = h_ref[...]
    mod = mod_ref[...]
    xn = _rms(x, g_ref[...]) * (1.0 + mod[k0 + 1:k0 + 2]) + mod[k0:k0 + 1]
    xb = xn.astype(BF16)
    acc = None
    for f0, f1 in tf:
        a = _dot(xb, w13_ref[:, f0:f1])
        b = _dot(xb, w13_ref[:, ff + f0:ff + f1])
        act = (a * _sigmoid(a) * b).astype(BF16)
        part = _dot(act, w2_ref[f0:f1, :])
        acc = part if acc is None else acc + part
    out = x + (0.5 * mod[k0 + 2:k0 + 3]) * acc
    if final:
        out = _rms(out, fg_ref[...])
    o_ref[...] = out


def _ffn(h, mods, l, mrow, g, w13, w2, k0, final_g=None):
    B, R, D = h.shape
    ff = w2.shape[1]
    tm = min(FFN_TM, R)
    cut = (ff // MXU_COLS // 2) * MXU_COLS
    tf = ((0, cut), (cut, ff)) if 0 < cut < ff else ((0, ff),)
    final = final_g is not None
    in_specs = [pl.BlockSpec((None, tm, D), lambda b, i: (b, i, 0)),
                _mod_spec(mods, l, mrow),
                _layer(g, l), _layer(w13, l), _layer(w2, l)]
    args = [h, mods, g, w13, w2]
    if final:
        in_specs.append(_resident((1, D)))
        args.append(final_g.reshape(1, D))
    return pl.pallas_call(
        functools.partial(_ffn_kernel, k0=k0, ff=ff, tf=tf, final=final),
        grid=(B, R // tm),
        in_specs=in_specs,
        out_specs=pl.BlockSpec((None, tm, D), lambda b, i: (b, i, 0)),
        out_shape=jax.ShapeDtypeStruct((B, R, D), F32),
        compiler_params=_params(("arbitrary", "arbitrary")),
        name="ffn",
    )(*args)


_O_CQ = 0
_O_GV = _O_CQ + MLA_Q_RANK
_O_CKV = _O_GV + GQA_KV_HEADS * GQA_HEAD_DIM
_O_U = _O_CKV + MLA_KV_RANK
_O_GQ = _O_U + SSM_WIDTH
_O_GK = _O_GQ + GQA_HEADS * GQA_HEAD_DIM
_O_KR = _O_GK + GQA_KV_HEADS * GQA_HEAD_DIM
_O_GATE = _O_KR + LANES
_QW = MLA_HEADS * HEAD_BLK
_VW = MLA_HEADS * MLA_V
_GQW = GQA_HEADS * GQA_HEAD_DIM
_GKW = GQA_KV_HEADS * GQA_HEAD_DIM


_PIECES = LANES // SSM_GROUP
_ZW = CHUNK * SSM_GROUP


def _piece_masks(rows):
    lane = lax.broadcasted_iota(jnp.int32, (rows, LANES), 1)
    return [(lane >= i * SSM_GROUP) & (lane < (i + 1) * SSM_GROUP) for i in range(_PIECES)]


def _block_transpose(a, masks):
    out = []
    for d in range(_PIECES):
        acc = None
        for s in range(_PIECES):
            shift = ((s - d) % _PIECES) * SSM_GROUP
            r = pltpu.roll(a[s], shift, 1) if shift else a[s]
            acc = r if acc is None else jnp.where(masks[s], r, acc)
        out.append(acc)
    return out


def _inproj_kernel(h_ref, mod_ref, g_ref, w1_ref, qn_ref, kvn_ref, wq_ref, wkv_ref, e_ref, tabm_ref, tabg_ref,
                   q_ref, k_ref, v_ref, z_ref, gq_ref, gk_ref, gv_ref, gate_ref, us_ref, *, d_model):
    x = h_ref[...]
    mod = mod_ref[...]
    xb = (_rms(x, g_ref[...]) * (1.0 + mod[4:5]) + mod[3:4]).astype(BF16)

    def proj(o, n):
        return _dot(xb, w1_ref[:, o:o + n])

    lane = lax.broadcasted_iota(jnp.int32, (x.shape[0], LANES), 1)

    def rope(t, n_half, cos, sin_signed):
        first = (lane & (2 * n_half - 1)) < n_half
        partner = jnp.where(first, pltpu.roll(t, LANES - n_half, 1), pltpu.roll(t, n_half, 1))
        return t * cos + partner * sin_signed

    cq_gv = proj(_O_CQ, MLA_Q_RANK + _GKW)
    gv_ref[...] = cq_gv[:, MLA_Q_RANK:].astype(BF16)
    cqn = _rms(cq_gv[:, :MLA_Q_RANK], qn_ref[...]).astype(BF16)
    qq = _dot(cqn, wq_ref[...])
    cos_q = tabm_ref[:, 0:LANES]
    sin_q = tabm_ref[:, LANES:2 * LANES]
    for h in range(MLA_HEADS):
        hs = slice(h * HEAD_BLK, (h + 1) * HEAD_BLK)
        q_ref[:, hs] = rope(qq[:, hs], MLA_ROPE // 4, cos_q, sin_q).astype(BF16)
    ckvn = _rms(proj(_O_CKV, MLA_KV_RANK), kvn_ref[...]).astype(BF16)
    gk_kr = proj(_O_GK, _GKW + LANES)
    kr = rope(gk_kr[:, _GKW:], MLA_ROPE // 4, tabm_ref[:, 2 * LANES:3 * LANES], tabm_ref[:, 3 * LANES:4 * LANES])
    k_ref[...] = (_dot(ckvn, wkv_ref[:, 0:_QW]) + _dot(kr.astype(BF16), e_ref[...])).astype(BF16)
    v_ref[...] = _dot(ckvn, wkv_ref[:, _QW:_QW + _VW]).astype(BF16)
    u = proj(_O_U, SSM_WIDTH)
    nr = us_ref.shape[1] // CHUNK
    masks = _piece_masks(nr)
    for lt in range(SSM_WIDTH // LANES):
        us_ref[lt] = u[:, lt * LANES:(lt + 1) * LANES]
        for jh in range(_ZW // LANES):
            a = [us_ref[lt, pl.ds(jh * _PIECES + s, nr, stride=CHUNK), :] for s in range(_PIECES)]
            b = _block_transpose(a, masks)
            for d in range(_PIECES):
                z_ref[lt * _PIECES + d, :, jh * LANES:(jh + 1) * LANES] = b[d].astype(BF16)
    cos_gq = tabg_ref[:, 0:LANES]
    sin_gq = tabg_ref[:, LANES:2 * LANES]
    cos_gk = tabg_ref[:, 2 * LANES:3 * LANES]
    sin_gk = tabg_ref[:, 3 * LANES:4 * LANES]
    gq = proj(_O_GQ, _GQW)
    for j in range(_GQW // LANES):
        sl = slice(j * LANES, (j + 1) * LANES)
        gq_ref[:, sl] = rope(gq[:, sl], GQA_HEAD_DIM // 4, cos_gq, sin_gq).astype(BF16)
    gk_ref[...] = rope(gk_kr[:, :_GKW], GQA_HEAD_DIM // 4, cos_gk, sin_gk).astype(BF16)
    gate_ref[...] = _sigmoid(proj(_O_GATE, N_BRANCH * d_model)).astype(BF16)


def _inproj(h, mods, l, mrow, g, lw, e_mat, tabm, tabg):
    B, R, D = h.shape
    tm = min(512, R)
    widths = (_QW, _QW, _VW, None, _GQW, _GKW, _GKW, N_BRANCH * D)
    row = lambda w: pl.BlockSpec((None, tm, w), lambda b, i: (b, i, 0))
    zspec = pl.BlockSpec((None, SSM_GROUPS, tm // CHUNK, _ZW), lambda b, i: (b, 0, i, 0))
    zshape = jax.ShapeDtypeStruct((B, SSM_GROUPS, R // CHUNK, _ZW), BF16)
    return pl.pallas_call(
        functools.partial(_inproj_kernel, d_model=D),
        grid=(B, R // tm),
        in_specs=[row(D),
                  _mod_spec(mods, l, mrow),
                  _layer(g, l), _layer(lw["w1"], l), _layer(lw["qn"], l), _layer(lw["kvn"], l),
                  _layer(lw["wq"], l), _layer(lw["wkv"], l),
                  _resident(e_mat.shape),
                  pl.BlockSpec((tm, 4 * LANES), lambda b, i: (i, 0)),
                  pl.BlockSpec((tm, 4 * LANES), lambda b, i: (i, 0))],
        out_specs=[zspec if w is None else row(w) for w in widths],
        out_shape=[zshape if w is None else jax.ShapeDtypeStruct((B, R, w), BF16) for w in widths],
        scratch_shapes=[pltpu.VMEM((SSM_WIDTH // LANES, tm, LANES), F32)],
        compiler_params=_params(("arbitrary", "arbitrary")),
        name="inproj",
    )(h, mods, g, lw["w1"], lw["qn"], lw["kvn"], lw["wq"], lw["wkv"], e_mat, tabm, tabg)


def _mla_kernel(q_ref, kc_ref, vc_ref, *rest, tk, n_chunks):
    if n_chunks:
        k_ref, v_ref, o_ref = rest
    else:
        (o_ref,) = rest
    tq = q_ref.shape[0]

    def tile(kb, vb, carry):
        lane_v = lax.broadcasted_iota(jnp.int32, vb.shape, 1)
        one = jnp.ones_like(vb)
        vbs = (jnp.where(lane_v < MLA_V, vb, one), jnp.where(lane_v >= MLA_V, vb, one))
        out = []
        for hh in range(2):
            m, acc = carry[hh]
            hs = slice(hh * HEAD_BLK, (hh + 1) * HEAD_BLK)
            s = _dot_nt(q_ref[:, hs], kb[:, hs])
            m_new = jnp.maximum(m, jnp.max(s, axis=-1, keepdims=True))
            alpha = jnp.exp2(m - m_new)
            p = jnp.exp2(s - m_new)
            acc = alpha * acc + _dot(p.astype(BF16), vbs[hh])
            out.append((m_new, acc))
        return tuple(out)

    init = tuple((jnp.full((tq, 1), NEG_INF, F32), jnp.zeros((tq, LANES), F32)) for _ in range(2))
    carry = tile(kc_ref[...], vc_ref[...], init)
    if n_chunks:
        def body(j, c):
            r = pl.ds(pl.multiple_of(j * tk, tk), tk)
            return tile(k_ref[r, :], v_ref[r, :], c)
        carry = lax.fori_loop(0, n_chunks, body, carry, unroll=True)
    lane = lax.broadcasted_iota(jnp.int32, (tq, LANES), 1)
    a0, a1 = carry[0][1], carry[1][1]
    o_ref[...] = jnp.where(lane < MLA_V, a0 / pltpu.roll(a0, MLA_V, 1), a1 / pltpu.roll(a1, MLA_V, 1)).astype(BF16)


def _mla(q, kc, vc, k=None, v=None):
    B, Tq, _ = q.shape
    C = kc.shape[1]
    tq = min(MLA_TQ, Tq)
    nh2 = MLA_HEADS // 2
    in_specs = [pl.BlockSpec((None, tq, 2 * HEAD_BLK), lambda b, h, i: (b, i, h)),
                pl.BlockSpec((None, C, 2 * HEAD_BLK), lambda b, h, i: (b, 0, h)),
                pl.BlockSpec((None, C, 2 * MLA_V), lambda b, h, i: (b, 0, h))]
    args = [q, kc, vc]
    n_chunks, tk = 0, 0
    if k is not None:
        T = k.shape[1]
        tk = min(MLA_TK, T)
        n_chunks = T // tk
        in_specs += [pl.BlockSpec((None, T, 2 * HEAD_BLK), lambda b, h, i: (b, 0, h)),
                     pl.BlockSpec((None, T, 2 * MLA_V), lambda b, h, i: (b, 0, h))]
        args += [k, v]
    return pl.pallas_call(
        functools.partial(_mla_kernel, tk=tk, n_chunks=n_chunks),
        grid=(B, nh2, Tq // tq),
        in_specs=in_specs,
        out_specs=pl.BlockSpec((None, tq, 2 * MLA_V), lambda b, h, i: (b, i, h)),
        out_shape=jax.ShapeDtypeStruct((B, Tq, _VW), BF16),
        compiler_params=_params(("arbitrary", "arbitrary", "arbitrary")),
        name="mla",
    )(*args)


def _gqa_kernel(sink_ref, q_ref, kc_ref, vc_ref, *rest, tq, band, seq, layer):
    if band:
        k_ref, v_ref, o_ref = rest
    else:
        (o_ref,) = rest
    half = GQA_HEAD_DIM
    lane = lax.broadcasted_iota(jnp.int32, (tq, LANES), 1)
    kc = kc_ref[...]
    vc = vc_ref[...]
    if band:
        nk = tq + 2 * WINDOW
        q0 = pl.program_id(1) * tq
        start = pl.multiple_of(jnp.clip(q0 - WINDOW, 0, seq - nk), LANES)
        kw = k_ref[pl.ds(start, nk), :]
        vw = v_ref[pl.ds(start, nk), :]
        r = lax.broadcasted_iota(jnp.int32, (2 * tq, nk), 0)
        qpos = q0 + jnp.where(r < tq, r, r - tq)
        kpos = start + lax.broadcasted_iota(jnp.int32, (2 * tq, nk), 1)
        valid = jnp.abs(qpos - kpos) <= WINDOW
    rows = lax.broadcasted_iota(jnp.int32, (2 * tq, 1), 0)
    for j in range(GQA_HEADS // 2):
        qb = q_ref[:, j * LANES:(j + 1) * LANES]
        zero = jnp.zeros_like(qb)
        qq = jnp.concatenate([jnp.where(lane < half, qb, zero), jnp.where(lane >= half, qb, zero)], axis=0)
        sink = jnp.where(rows < tq, sink_ref[layer, j], sink_ref[layer, GQA_HEADS // 2 + j])
        s_c = _dot_nt(qq, kc)
        m = jnp.maximum(sink, jnp.max(s_c, axis=-1, keepdims=True))
        if band:
            s_b = jnp.where(valid, _dot_nt(qq, kw), NEG_INF)
            m = jnp.maximum(m, jnp.max(s_b, axis=-1, keepdims=True))
        p_c = jnp.exp2(s_c - m)
        den = jnp.exp2(sink - m) + jnp.sum(p_c, axis=-1, keepdims=True)
        o = _dot(p_c.astype(BF16), vc)
        if band:
            p_b = jnp.exp2(s_b - m)
            den = den + jnp.sum(p_b, axis=-1, keepdims=True)
            o = o + _dot(p_b.astype(BF16), vw)
        o = o / den
        o_ref[:, j * LANES:(j + 1) * LANES] = jnp.where(lane < half, o[:tq], o[tq:]).astype(BF16)


def _gqa(sink, l, q, kc, vc, k=None, v=None):
    B, Tq, _ = q.shape
    C = kc.shape[1]
    band = k is not None
    tq = min(GQA_TQ, Tq)
    full = lambda n: pl.BlockSpec((None, n, _GKW), lambda b, i: (b, 0, 0))
    in_specs = [pl.BlockSpec(memory_space=pltpu.SMEM),
                pl.BlockSpec((None, tq, _GQW), lambda b, i: (b, i, 0)),
                full(C), full(C)]
    args = [sink, q, kc, vc]
    seq = 0
    if band:
        seq = k.shape[1]
        assert seq >= tq + 2 * WINDOW
        in_specs += [full(seq), full(seq)]
        args += [k, v]
    return pl.pallas_call(
        functools.partial(_gqa_kernel, tq=tq, band=band, seq=seq, layer=l),
        grid=(B, Tq // tq),
        in_specs=in_specs,
        out_specs=pl.BlockSpec((None, tq, _GQW), lambda b, i: (b, i, 0)),
        out_shape=jax.ShapeDtypeStruct((B, Tq, _GQW), BF16),
        compiler_params=_params(("arbitrary", "arbitrary")),
        name="gqa",
    )(*args)


def _gelu(y):
    return 0.5 * y * (1.0 + jnp.tanh(math.sqrt(2.0 / math.pi) * (y + 0.044715 * (y * y * y))))


def _ssm_kernel(zc_ref, zl_ref, we_ref, wy_ref, a_ref, yc_ref, yl_ref, e_ref, sf_ref, sr_ref,
                *, n_ctx, n_lat, batch):
    sw = PAIR * SSM_STATE
    n_all = n_ctx + n_lat
    segs = ((zc_ref, yc_ref, 0, n_ctx), (zl_ref, yl_ref, n_ctx, n_lat))

    def chunk_rows(c0, n, pp, b):
        return pl.ds(c0 * SUBLANES + pp * batch + b, n, stride=SUBLANES)

    def z_pair(z_ref, b, pp):
        return jnp.concatenate([z_ref[b, PAIR * pp], z_ref[b, PAIR * pp + 1]], axis=1)

    for z_ref, _, c0, n in segs:
        for pp in range(PAIR):
            for b in range(batch):
                e = _dot(z_pair(z_ref, b, pp), we_ref[pp])
                for k in range(4):
                    e_ref[k, chunk_rows(c0, n, pp, b), :] = e[:, k * sw:(k + 1) * sw]
    af_re, af_im, ar_re, ar_im = a_ref[0], a_ref[1], a_ref[2], a_ref[3]
    zero = jnp.zeros((SUBLANES, sw), F32)

    def rows_of(c):
        return pl.ds(pl.multiple_of(c * SUBLANES, SUBLANES), SUBLANES)

    def fwd(c, carry):
        s_re, s_im = carry
        r = rows_of(c)
        sf_ref[0, r, :] = s_re
        sf_ref[1, r, :] = s_im
        return (af_re * s_re - af_im * s_im + e_ref[0, r, :],
                af_re * s_im + af_im * s_re + e_ref[1, r, :])

    lax.fori_loop(0, n_all, fwd, (zero, zero))

    def rev_at(c, carry):
        s_re, s_im = carry
        r = rows_of(c)
        sr_ref[0, r, :] = s_re
        sr_ref[1, r, :] = s_im
        return (ar_re * s_re - ar_im * s_im + e_ref[2, r, :],
                ar_re * s_im + ar_im * s_re + e_ref[3, r, :])

    carry = lax.fori_loop(0, n_ctx, lambda i, c: rev_at(n_ctx - 1 - i, c), (zero, zero))
    lax.fori_loop(0, n_lat, lambda i, c: rev_at(n_all - 1 - i, c), carry)

    uw = PAIR * _ZW
    for z_ref, y_ref, c0, n in segs:
        for pp in range(PAIR):
            for b in range(batch):
                r = chunk_rows(c0, n, pp, b)
                s_f = jnp.concatenate([sf_ref[0, r, :], sf_ref[1, r, :]], axis=1).astype(BF16)
                s_r = jnp.concatenate([sr_ref[0, r, :], sr_ref[1, r, :]], axis=1).astype(BF16)
                y = (_dot(z_pair(z_ref, b, pp), wy_ref[pp, 0:uw, :])
                     + _dot(s_f, wy_ref[pp, uw:uw + 2 * sw, :])
                     + _dot(s_r, wy_ref[pp, uw + 2 * sw:uw + 4 * sw, :]))
                y = _gelu(y).astype(BF16)
                y_ref[b, PAIR * pp] = y[:, :_ZW]
                y_ref[b, PAIR * pp + 1] = y[:, _ZW:]


def _ssm(z_lat, z_ctx, we, wy, a, l):
    B, G, n_lat, _ = z_lat.shape
    n_ctx = z_ctx.shape[2]
    assert PAIR * B == SUBLANES, "scan rows pack (pair, batch) into one sublane tile"
    gstep = PAIR * PAIR
    rows = (n_ctx + n_lat) * SUBLANES
    sw2 = 2 * PAIR * SSM_STATE
    uw = PAIR * _ZW
    zspec = lambda n: pl.BlockSpec((B, gstep, n, _ZW), lambda g: (0, g, 0, 0))
    y_ctx, y_lat = pl.pallas_call(
        functools.partial(_ssm_kernel, n_ctx=n_ctx, n_lat=n_lat, batch=B),
        grid=(G // gstep,),
        in_specs=[zspec(n_ctx), zspec(n_lat),
                  pl.BlockSpec((None, PAIR, uw, 2 * sw2), lambda g: (l, g, 0, 0)),
                  pl.BlockSpec((None, PAIR, uw + 2 * sw2, uw), lambda g: (l, g, 0, 0)),
                  pl.BlockSpec((None, None, 4, SUBLANES, PAIR * SSM_STATE), lambda g: (l, g, 0, 0, 0))],
        out_specs=[zspec(n_ctx), zspec(n_lat)],
        out_shape=[jax.ShapeDtypeStruct(z_ctx.shape, BF16), jax.ShapeDtypeStruct(z_lat.shape, BF16)],
        scratch_shapes=[pltpu.VMEM((4, rows, PAIR * SSM_STATE), F32),
                        pltpu.VMEM((2, rows, PAIR * SSM_STATE), F32),
                        pltpu.VMEM((2, rows, PAIR * SSM_STATE), F32)],
        compiler_params=_params(("arbitrary",)),
        name="ssm",
    )(z_ctx, z_lat, we, wy, a)
    return y_lat, y_ctx


def _merge_kernel(h_ref, mod_ref, a_ref, yg_ref, g_ref, gate_ref, wo_ref, wglu_ref, wgo_ref, wout_ref, o_ref,
                  ys_ref, *, d_model):
    D = d_model
    b0 = _dot(a_ref[...], wo_ref[...])
    nr = ys_ref.shape[1] // CHUNK
    masks = _piece_masks(nr)
    for lt in range(SSM_WIDTH // LANES):
        for jh in range(_ZW // LANES):
            a = [yg_ref[lt * _PIECES + s, :, jh * LANES:(jh + 1) * LANES].astype(F32) for s in range(_PIECES)]
            b = _block_transpose(a, masks)
            for d in range(_PIECES):
                ys_ref[lt, pl.ds(jh * _PIECES + d, nr, stride=CHUNK), :] = b[d]
    ys = jnp.concatenate([ys_ref[lt] for lt in range(SSM_WIDTH // LANES)], axis=1)
    t = _dot(ys.astype(BF16), wglu_ref[...])
    b1 = t[:, :D] * _sigmoid(t[:, D:])
    b2 = _dot(g_ref[...], wgo_ref[...])
    gate = gate_ref[...].astype(F32)
    mix = gate[:, 0:D] * b0 + gate[:, D:2 * D] * b1 + gate[:, 2 * D:3 * D] * b2
    o_ref[...] = h_ref[...] + mod_ref[5:6, :] * _dot(mix.astype(BF16), wout_ref[...])


def _merge(h, mods, l, mrow, attn, yssm, gqa, gate, lw):
    B, R, D = h.shape
    tm = min(512, R)
    row = lambda w: pl.BlockSpec((None, tm, w), lambda b, i: (b, i, 0))
    return pl.pallas_call(
        functools.partial(_merge_kernel, d_model=D),
        grid=(B, R // tm),
        in_specs=[row(D),
                  _mod_spec(mods, l, mrow),
                  row(_VW),
                  pl.BlockSpec((None, SSM_GROUPS, tm // CHUNK, _ZW), lambda b, i: (b, 0, i, 0)),
                  row(_GQW), row(N_BRANCH * D),
                  _layer(lw["wo"], l), _layer(lw["wglu"], l), _layer(lw["wgo"], l), _layer(lw["wout"], l)],
        out_specs=row(D),
        out_shape=jax.ShapeDtypeStruct((B, R, D), F32),
        scratch_shapes=[pltpu.VMEM((SSM_WIDTH // LANES, tm, LANES), F32)],
        compiler_params=_params(("arbitrary", "arbitrary")),
        name="merge",
    )(h, mods, attn, yssm, gqa, gate, lw["wo"], lw["wglu"], lw["wgo"], lw["wout"])


def _rope_angles(T, n):
    t = np.arange(T)
    inv = ROPE_BASE ** (-np.arange(0, n, 2, dtype=np.float32) / n)
    out = []
    for pos in (t // GRID_W, t % GRID_W):
        ang = jnp.asarray(pos.astype(np.float32))[:, None] * jnp.asarray(inv)[None, :]
        out.append((jnp.cos(ang), jnp.sin(ang)))
    cos = jnp.concatenate([out[0][0], out[0][0], out[1][0], out[1][0]], axis=1)
    sin = jnp.concatenate([-out[0][1], out[0][1], -out[1][1], out[1][1]], axis=1)
    return cos, sin


def _tables(T, C):
    sc_m = (MLA_NOPE + MLA_ROPE) ** -0.5 * LOG2E
    sc_g = GQA_HEAD_DIM ** -0.5 * LOG2E
    cm, sm = _rope_angles(T, MLA_ROPE // 2)
    cg, sg = _rope_angles(T, GQA_HEAD_DIM // 2)
    one = lambda n, w: jnp.ones((n, w), F32)
    zero = lambda n, w: jnp.zeros((n, w), F32)
    pad = LANES - MLA_NOPE - MLA_ROPE

    def mla_tab(n, c, s):
        cos_q = jnp.concatenate([one(n, MLA_NOPE), c, zero(n, pad)], axis=1) * sc_m
        sin_q = jnp.concatenate([zero(n, MLA_NOPE), s, zero(n, pad)], axis=1) * sc_m
        kr_cos = jnp.concatenate([c, zero(n, LANES - MLA_ROPE)], axis=1)
        kr_sin = jnp.concatenate([s, zero(n, LANES - MLA_ROPE)], axis=1)
        return jnp.concatenate([cos_q, sin_q, kr_cos, kr_sin], axis=1)

    def gqa_tab(n, c, s):
        c2, s2 = jnp.concatenate([c, c], axis=1), jnp.concatenate([s, s], axis=1)
        return jnp.concatenate([c2 * sc_g, s2 * sc_g, c2, s2], axis=1)

    lat = (mla_tab(T, cm, sm), gqa_tab(T, cg, sg))
    ctx = (mla_tab(C, one(C, MLA_ROPE), zero(C, MLA_ROPE)),
           gqa_tab(C, one(C, GQA_HEAD_DIM), zero(C, GQA_HEAD_DIM)))
    return lat, ctx


def _place_matrix():
    e = np.zeros((LANES, _QW), np.float32)
    for h in range(MLA_HEADS):
        for j in range(MLA_ROPE):
            e[j, h * HEAD_BLK + MLA_NOPE + j] = 1.0
    return jnp.asarray(e, BF16)


def _layer_weights(w_in, q_norm, kv_norm, w_uq, w_ukv, w_o, w_glu, gqa_w_o, w_out):
    D = w_in.shape[0]
    offs = np.cumsum([MLA_Q_RANK, MLA_KV_RANK, MLA_ROPE, SSM_WIDTH, _GQW, _GKW, _GKW]).tolist()
    cq, ckv, kr, u, gq, gk, gv, gates = jnp.split(w_in, offs, axis=1)
    order = np.asarray(GQA_ORDER)
    gq_h = gq.reshape(D, GQA_HEADS, GQA_HEAD_DIM)
    kr_blk = jnp.concatenate([kr, jnp.zeros((D, LANES - MLA_ROPE), F32)], axis=1)
    w1 = jnp.concatenate([cq, gv, ckv, u, gq_h[:, order].reshape(D, _GQW), gk, kr_blk, gates], axis=1).astype(BF16)
    uq = w_uq.reshape(MLA_Q_RANK, MLA_HEADS, MLA_NOPE + MLA_ROPE)
    zpad = jnp.zeros((MLA_Q_RANK, MLA_HEADS, HEAD_BLK - MLA_NOPE - MLA_ROPE), F32)
    wq = jnp.concatenate([uq, zpad], axis=-1).reshape(MLA_Q_RANK, _QW).astype(BF16)
    ukv = w_ukv.reshape(MLA_KV_RANK, MLA_HEADS, MLA_NOPE + MLA_V)
    wk = jnp.concatenate([ukv[..., :MLA_NOPE], jnp.zeros((MLA_KV_RANK, MLA_HEADS, HEAD_BLK - MLA_NOPE), F32)],
                         axis=-1).reshape(MLA_KV_RANK, _QW)
    wv = ukv[..., MLA_NOPE:].reshape(MLA_KV_RANK, _VW)
    wkv = jnp.concatenate([wk, wv], axis=1).astype(BF16)
    wgo = gqa_w_o.reshape(GQA_HEADS, GQA_HEAD_DIM, D)[order].reshape(_GQW, D).astype(BF16)
    return dict(w1=w1, qn=q_norm.reshape(1, -1), kvn=kv_norm.reshape(1, -1), wq=wq, wkv=wkv,
                wo=w_o.astype(BF16), wglu=w_glu.astype(BF16), wgo=wgo, wout=w_out.astype(BF16))


def _ssm_weights(lam_re, lam_im, log_dt, b_re, b_im, c_re, c_im, d_skip):
    G, P, M, Lc = SSM_GROUPS, SSM_STATE, SSM_GROUP, CHUNK
    dt = jnp.exp(log_dt)[..., None]
    kk = jnp.arange(Lc + 1, dtype=F32)[:, None, None, None]
    mag = jnp.exp(lam_re[None] * dt[None] * kk)
    pw_re, pw_im = mag * jnp.cos(lam_im[None] * dt[None] * kk), mag * jnp.sin(lam_im[None] * dt[None] * kk)
    a_re, a_im = pw_re[1], pw_im[1]
    den = lam_re * lam_re + lam_im * lam_im
    w_re = ((a_re - 1) * lam_re + a_im * lam_im) / den
    w_im = (a_im * lam_re - (a_re - 1) * lam_im) / den
    bb_re = w_re[..., None] * b_re - w_im[..., None] * b_im
    bb_im = w_re[..., None] * b_im + w_im[..., None] * b_re
    ca_re = c_re[:, None] * jnp.moveaxis(pw_re, 0, 1)[:, :, :, None, :] - c_im[:, None] * jnp.moveaxis(pw_im, 0, 1)[:, :, :, None, :]
    ca_im = c_re[:, None] * jnp.moveaxis(pw_im, 0, 1)[:, :, :, None, :] + c_im[:, None] * jnp.moveaxis(pw_re, 0, 1)[:, :, :, None, :]
    cb = jnp.sum(ca_re[..., None] * bb_re[:, None, :, None] - ca_im[..., None] * bb_im[:, None, :, None], axis=-2)
    lags_f = jnp.transpose(cb[0, :Lc], (1, 3, 0, 2)).reshape(G, M, Lc * M)
    lags_r = jnp.transpose(cb[1, :Lc][::-1], (1, 3, 0, 2)).reshape(G, M, Lc * M)
    halo = jnp.zeros((G, M, (Lc - 1) * M), F32)
    ext = jnp.concatenate([halo, lags_f], axis=-1) + jnp.concatenate([lags_r, halo], axis=-1)
    kmat = jnp.stack([ext[:, :, (Lc - 1 - i) * M:(Lc - 1 - i) * M + Lc * M] for i in range(Lc)], axis=1)
    skip = jnp.tile(d_skip.reshape(G, M), (1, Lc))
    kmat = kmat.reshape(G, Lc * M, Lc * M) + jnp.eye(Lc * M, dtype=F32)[None] * skip[:, None, :]
    idx_f = np.arange(Lc)[::-1].copy()
    idx_r = np.arange(Lc)

    def drive(d, idx):
        p_re, p_im = pw_re[idx, d], pw_im[idx, d]
        e_re = p_re[..., None] * bb_re[d][None] - p_im[..., None] * bb_im[d][None]
        e_im = p_re[..., None] * bb_im[d][None] + p_im[..., None] * bb_re[d][None]
        to = lambda t: jnp.transpose(t, (1, 0, 3, 2)).reshape(G, Lc * M, P)
        return to(e_re), to(e_im)

    ef_re, ef_im = drive(0, idx_f)
    er_re, er_im = drive(1, idx_r)

    def readout(d, idx):
        q_re, q_im = ca_re[d][idx], ca_im[d][idx]
        to = lambda t: jnp.transpose(t, (1, 3, 0, 2)).reshape(G, P, Lc * M)
        return to(q_re), to(-q_im)

    qf_re, qf_im = readout(0, np.arange(1, Lc + 1))
    qr_re, qr_im = readout(1, Lc - np.arange(Lc))
    npair = G // PAIR

    def pair_diag(t):
        R, Cn = t.shape[1:]
        t = t.reshape(npair, PAIR, R, Cn)
        z = jnp.zeros((npair, R, Cn), F32)
        rows = [jnp.concatenate([t[:, p] if q == p else z for q in range(PAIR)], axis=-1) for p in range(PAIR)]
        return jnp.concatenate(rows, axis=1)

    we = jnp.concatenate([pair_diag(ef_re), pair_diag(ef_im), pair_diag(er_re), pair_diag(er_im)], axis=2)
    wy = jnp.concatenate([pair_diag(kmat), pair_diag(qf_re), pair_diag(qf_im), pair_diag(qr_re), pair_diag(qr_im)], axis=1)
    sw = PAIR * P
    nstep = npair // PAIR

    def per_row(t):
        t = t.reshape(nstep, PAIR, 1, sw)
        return jnp.broadcast_to(t, (nstep, PAIR, SUBLANES // PAIR, sw)).reshape(nstep, SUBLANES, sw)

    a_step = jnp.stack([per_row(pw_re[Lc, 0]), per_row(pw_im[Lc, 0]), per_row(pw_re[Lc, 1]), per_row(pw_im[Lc, 1])],
                       axis=1)
    return we.astype(BF16), wy.astype(BF16), a_step


def kernel(x, c, ctx, c_ctx, ada_w, ada_b, norm_ffn1, norm_mix, norm_ffn2, ffn1_w13, ffn1_w2, ffn2_w13, ffn2_w2, w_in, mla_q_norm, mla_kv_norm, mla_w_uq, mla_w_ukv, mla_w_o, ssm_lambda_re, ssm_lambda_im, ssm_log_dt, ssm_b_re, ssm_b_im, ssm_c_re, ssm_c_im, ssm_d, ssm_w_glu, gqa_sink, gqa_w_o, w_out, final_norm):
    B, T, D = x.shape
    C = ctx.shape[1]
    depth = ada_w.shape[0]
    cc = jnp.concatenate([c, c_ctx[None], jnp.zeros((SUBLANES - B - 1, D), F32)], axis=0)
    mods = _ada(cc, ada_w, ada_b).reshape(depth, SUBLANES, N_MOD, D)
    (tabm, tabg), (tabm_c, tabg_c) = _tables(T, C)
    sinks = gqa_sink * LOG2E
    lw = jax.vmap(_layer_weights)(w_in, mla_q_norm, mla_kv_norm, mla_w_uq, mla_w_ukv, mla_w_o, ssm_w_glu, gqa_w_o,
                                  w_out)
    we, wy, a_step = jax.vmap(_ssm_weights)(ssm_lambda_re, ssm_lambda_im, ssm_log_dt, ssm_b_re, ssm_b_im,
                                            ssm_c_re, ssm_c_im, ssm_d)
    e_mat = _place_matrix()
    ffn1 = (norm_ffn1[:, None, :], ffn1_w13.astype(BF16), ffn1_w2.astype(BF16))
    ffn2 = (norm_ffn2[:, None, :], ffn2_w13.astype(BF16), ffn2_w2.astype(BF16))
    g_mix = norm_mix[:, None, :]
    lat, cx = None, B
    h, hc = x, ctx
    for l in range(depth):
        ctx_out = l < depth - 1
        last = l == depth - 1
        h = _ffn(h, mods, l, lat, *ffn1, 0)
        hc = _ffn(hc, mods, l, cx, *ffn1, 0)
        q, k, v, z, gq, gk, gv, gate = _inproj(h, mods, l, lat, g_mix, lw, e_mat, tabm, tabg)
        q_c, k_c, v_c, z_c, gq_c, gk_c, gv_c, gate_c = _inproj(hc, mods, l, cx, g_mix, lw, e_mat, tabm_c, tabg_c)
        attn = _mla(q, k_c, v_c, k, v)
        y_lat, y_ctx = _ssm(z, z_c, we, wy, a_step, l)
        gqa = _gqa(sinks, l, gq, gk_c, gv_c, gk, gv)
        h = _merge(h, mods, l, lat, attn, y_lat, gqa, gate, lw)
        h = _ffn(h, mods, l, lat, *ffn2, 6, final_norm if last else None)
        if ctx_out:
            attn_c = _mla(q_c, k_c, v_c)
            gqa_c = _gqa(sinks, l, gq_c, gk_c, gv_c)
            hc = _merge(hc, mods, l, cx, attn_c, y_ctx, gqa_c, gate_c, lw)
            hc = _ffn(hc, mods, l, cx, *ffn2, 6)
    return h
```

```python
import functools
import math

import numpy as np
import jax
import jax.numpy as jnp
from jax import lax
from jax.experimental import pallas as pl
from jax.experimental.pallas import tpu as pltpu

F32 = jnp.float32
BF16 = jnp.bfloat16

GRID_W = 64
MLA_HEADS = 8
MLA_NOPE = 64
MLA_ROPE = 32
MLA_V = 64
MLA_Q_RANK = 384
MLA_KV_RANK = 256
SSM_WIDTH = 512
SSM_GROUP = 16
SSM_GROUPS = SSM_WIDTH // SSM_GROUP
SSM_STATE = 64
GQA_HEADS = 8
GQA_KV_HEADS = 2
GQA_HEAD_DIM = 64
WINDOW = 128
N_BRANCH = 3
N_MOD = 9
ROPE_BASE = 10000.0
EPS = 1e-6
NEG_INF = -1e30
LOG2E = math.log2(math.e)

LANES = 128
SUBLANES = 8
VMEM_LIMIT = 56 * 1024 * 1024
CHUNK = 16
PAIR = 2
HEAD_BLK = LANES
MXU_COLS = 256
FFN_TM = 1024
GQA_TQ = 256
MLA_TQ = 1024
MLA_TK = 2048
GQA_ORDER = (0, 4, 1, 5, 2, 6, 3, 7)


def _dot(a, b):
    return jnp.dot(a, b, preferred_element_type=F32)


def _dot_nt(a, b):
    return lax.dot_general(a, b, (((1,), (1,)), ((), ())), preferred_element_type=F32)


def _rms(x, g):
    return x * lax.rsqrt(jnp.mean(x * x, axis=-1, keepdims=True) + EPS) * g


def _sigmoid(x):
    return 0.5 * jnp.tanh(0.5 * x) + 0.5


def _params(sem, vmem=VMEM_LIMIT):
    return pltpu.CompilerParams(dimension_semantics=sem, vmem_limit_bytes=vmem)


def _resident(shape):
    nd = len(shape)
    return pl.BlockSpec(shape, lambda *_: (0,) * nd, pipeline_mode=pl.Buffered(1))


def _layer(arr, l):
    nd = arr.ndim
    return pl.BlockSpec((None,) + arr.shape[1:], lambda *_: (l,) + (0,) * (nd - 1), pipeline_mode=pl.Buffered(1))


def _mod_spec(mods, l, mrow):
    blk = (None, None) + mods.shape[2:]
    if mrow is None:
        return pl.BlockSpec(blk, lambda b, i: (l, b, 0, 0))
    return pl.BlockSpec(blk, lambda b, i: (l, mrow, 0, 0))


def _ada_kernel(c_ref, w_ref, b_ref, o_ref):
    c = c_ref[...]
    s = c * _sigmoid(c)
    w = w_ref[...]
    s_hi = s.astype(BF16)
    s_lo = (s - s_hi.astype(F32)).astype(BF16)
    w_hi = w.astype(BF16)
    w_lo = (w - w_hi.astype(F32)).astype(BF16)
    o_ref[...] = _dot(s_hi, w_hi) + _dot(s_hi, w_lo) + _dot(s_lo, w_hi) + b_ref[...]


def _ada(cc, ada_w, ada_b):
    L, D, N = ada_w.shape
    tn = 1152 if N % 1152 == 0 else N
    R = cc.shape[0]
    return pl.pallas_call(
        _ada_kernel,
        grid=(L, N // tn),
        in_specs=[pl.BlockSpec((R, D), lambda l, j: (0, 0)),
                  pl.BlockSpec((None, D, tn), lambda l, j: (l, 0, j)),
                  pl.BlockSpec((None, 1, tn), lambda l, j: (l, 0, j))],
        out_specs=pl.BlockSpec((None, R, tn), lambda l, j: (l, 0, j)),
        out_shape=jax.ShapeDtypeStruct((L, R, N), F32),
        compiler_params=_params(("arbitrary", "arbitrary")),
        name="ada",
    )(cc, ada_w, ada_b.reshape(L, 1, N))


def _ffn_kernel(h_ref, mod_ref, g_ref, w13_ref, w2_ref, *rest, k0, ff, tf, final):
    if final:
        fg_ref, o_ref = rest
    else:
        (o_ref,) = rest
    x = h_ref[...]
    mod = mod_ref[...]
    xn = _rms(x, g_ref[...]) * (1.0 + mod[k0 + 1:k0 + 2]) + mod[k0:k0 + 1]
    xb = xn.astype(BF16)
    acc = None
    for f0, f1 in tf:
        a = _dot(xb, w13_ref[:, f0:f1])
        b = _dot(xb, w13_ref[:, ff + f0:ff + f1])
        act = (a * _sigmoid(a) * b).astype(BF16)
        part = _dot(act, w2_ref[f0:f1, :])
        acc = part if acc is None else acc + part
    out = x + (0.5 * mod[k0 + 2:k0 + 3]) * acc
    if final:
        out = _rms(out, fg_ref[...])
    o_ref[...] = out


def _ffn(h, mods, l, mrow, g, w13, w2, k0, final_g=None):
    B, R, D = h.shape
    ff = w2.shape[1]
    tm = min(FFN_TM, R)
    cut = (ff // MXU_COLS // 2) * MXU_COLS
    tf = ((0, cut), (cut, ff)) if 0 < cut < ff else ((0, ff),)
    final = final_g is not None
    in_specs = [pl.BlockSpec((None, tm, D), lambda b, i: (b, i, 0)),
                _mod_spec(mods, l, mrow),
                _layer(g, l), _layer(w13, l), _layer(w2, l)]
    args = [h, mods, g, w13, w2]
    if final:
        in_specs.append(_resident((1, D)))
        args.append(final_g.reshape(1, D))
    return pl.pallas_call(
        functools.partial(_ffn_kernel, k0=k0, ff=ff, tf=tf, final=final),
        grid=(B, R // tm),
        in_specs=in_specs,
        out_specs=pl.BlockSpec((None, tm, D), lambda b, i: (b, i, 0)),
        out_shape=jax.ShapeDtypeStruct((B, R, D), F32),
        compiler_params=_params(("arbitrary", "arbitrary")),
        name="ffn",
    )(*args)


_O_CQ = 0
_O_GV = _O_CQ + MLA_Q_RANK
_O_CKV = _O_GV + GQA_KV_HEADS * GQA_HEAD_DIM
_O_U = _O_CKV + MLA_KV_RANK
_O_GQ = _O_U + SSM_WIDTH
_O_GK = _O_GQ + GQA_HEADS * GQA_HEAD_DIM
_O_KR = _O_GK + GQA_KV_HEADS * GQA_HEAD_DIM
_O_GATE = _O_KR + LANES
_QW = MLA_HEADS * HEAD_BLK
_VW = MLA_HEADS * MLA_V
_GQW = GQA_HEADS * GQA_HEAD_DIM
_GKW = GQA_KV_HEADS * GQA_HEAD_DIM


_PIECES = LANES // SSM_GROUP
_ZW = CHUNK * SSM_GROUP


def _piece_masks(rows):
    piece = lax.broadcasted_iota(jnp.int32, (rows, LANES), 1) // SSM_GROUP
    return {b: (piece & b) != 0 for b in (4, 2, 1)}


def _block_transpose(a, masks):
    a = list(a)
    for b in (4, 2, 1):
        hi = masks[b]
        new = list(a)
        for s in range(_PIECES):
            if s & b == 0:
                t = s | b
                new[s] = jnp.where(hi, pltpu.roll(a[t], b * SSM_GROUP, 1), a[s])
                new[t] = jnp.where(hi, a[t], pltpu.roll(a[s], LANES - b * SSM_GROUP, 1))
        a = new
    return a


def _inproj_kernel(h_ref, mod_ref, g_ref, w1_ref, qn_ref, kvn_ref, wq_ref, wkv_ref, e_ref, tabm_ref, tabg_ref,
                   q_ref, k_ref, v_ref, z_ref, gq_ref, gk_ref, gv_ref, gate_ref, us_ref, *, d_model):
    x = h_ref[...]
    mod = mod_ref[...]
    xb = (_rms(x, g_ref[...]) * (1.0 + mod[4:5]) + mod[3:4]).astype(BF16)

    def proj(o, n):
        return _dot(xb, w1_ref[:, o:o + n])

    lane = lax.broadcasted_iota(jnp.int32, (x.shape[0], LANES), 1)

    def rope(t, n_half, cos, sin_signed):
        first = (lane & (2 * n_half - 1)) < n_half
        partner = jnp.where(first, pltpu.roll(t, LANES - n_half, 1), pltpu.roll(t, n_half, 1))
        return t * cos + partner * sin_signed

    cq_gv = proj(_O_CQ, MLA_Q_RANK + _GKW)
    gv_ref[...] = cq_gv[:, MLA_Q_RANK:].astype(BF16)
    cqn = _rms(cq_gv[:, :MLA_Q_RANK], qn_ref[...]).astype(BF16)
    qq = _dot(cqn, wq_ref[...])
    cos_q = tabm_ref[:, 0:LANES]
    sin_q = tabm_ref[:, LANES:2 * LANES]
    for h in range(MLA_HEADS):
        hs = slice(h * HEAD_BLK, (h + 1) * HEAD_BLK)
        q_ref[:, hs] = rope(qq[:, hs], MLA_ROPE // 4, cos_q, sin_q).astype(BF16)
    ckvn = _rms(proj(_O_CKV, MLA_KV_RANK), kvn_ref[...]).astype(BF16)
    gk_kr = proj(_O_GK, _GKW + LANES)
    kr = rope(gk_kr[:, _GKW:], MLA_ROPE // 4, tabm_ref[:, 2 * LANES:3 * LANES], tabm_ref[:, 3 * LANES:4 * LANES])
    k_ref[...] = (_dot(ckvn, wkv_ref[:, 0:_QW]) + _dot(kr.astype(BF16), e_ref[...])).astype(BF16)
    v_ref[...] = _dot(ckvn, wkv_ref[:, _QW:_QW + _VW]).astype(BF16)
    u = proj(_O_U, SSM_WIDTH)
    nr = us_ref.shape[1] // CHUNK
    masks = _piece_masks(nr)
    for lt in range(SSM_WIDTH // LANES):
        us_ref[lt] = u[:, lt * LANES:(lt + 1) * LANES]
        for jh in range(_ZW // LANES):
            a = [us_ref[lt, pl.ds(jh * _PIECES + s, nr, stride=CHUNK), :] for s in range(_PIECES)]
            b = _block_transpose(a, masks)
            for d in range(_PIECES):
                z_ref[lt * _PIECES + d, :, jh * LANES:(jh + 1) * LANES] = b[d].astype(BF16)
    cos_gq = tabg_ref[:, 0:LANES]
    sin_gq = tabg_ref[:, LANES:2 * LANES]
    cos_gk = tabg_ref[:, 2 * LANES:3 * LANES]
    sin_gk = tabg_ref[:, 3 * LANES:4 * LANES]
    gq = proj(_O_GQ, _GQW)
    for j in range(_GQW // LANES):
        sl = slice(j * LANES, (j + 1) * LANES)
        gq_ref[:, sl] = rope(gq[:, sl], GQA_HEAD_DIM // 4, cos_gq, sin_gq).astype(BF16)
    gk_ref[...] = rope(gk_kr[:, :_GKW], GQA_HEAD_DIM // 4, cos_gk, sin_gk).astype(BF16)
    gate_ref[...] = _sigmoid(proj(_O_GATE, N_BRANCH * d_model)).astype(BF16)


def _inproj(h, mods, l, mrow, g, lw, e_mat, tabm, tabg):
    B, R, D = h.shape
    tm = min(512, R)
    widths = (_QW, _QW, _VW, None, _GQW, _GKW, _GKW, N_BRANCH * D)
    row = lambda w: pl.BlockSpec((None, tm, w), lambda b, i: (b, i, 0))
    zspec = pl.BlockSpec((None, SSM_GROUPS, tm // CHUNK, _ZW), lambda b, i: (b, 0, i, 0))
    zshape = jax.ShapeDtypeStruct((B, SSM_GROUPS, R // CHUNK, _ZW), BF16)
    return pl.pallas_call(
        functools.partial(_inproj_kernel, d_model=D),
        grid=(B, R // tm),
        in_specs=[row(D),
                  _mod_spec(mods, l, mrow),
                  _layer(g, l), _layer(lw["w1"], l), _layer(lw["qn"], l), _layer(lw["kvn"], l),
                  _layer(lw["wq"], l), _layer(lw["wkv"], l),
                  _resident(e_mat.shape),
                  pl.BlockSpec((tm, 4 * LANES), lambda b, i: (i, 0)),
                  pl.BlockSpec((tm, 4 * LANES), lambda b, i: (i, 0))],
        out_specs=[zspec if w is None else row(w) for w in widths],
        out_shape=[zshape if w is None else jax.ShapeDtypeStruct((B, R, w), BF16) for w in widths],
        scratch_shapes=[pltpu.VMEM((SSM_WIDTH // LANES, tm, LANES), F32)],
        compiler_params=_params(("arbitrary", "arbitrary")),
        name="inproj",
    )(h, mods, g, lw["w1"], lw["qn"], lw["kvn"], lw["wq"], lw["wkv"], e_mat, tabm, tabg)


def _mla_kernel(q_ref, kc_ref, vc_ref, *rest, tk, n_chunks):
    if n_chunks:
        k_ref, v_ref, o_ref = rest
    else:
        (o_ref,) = rest
    tq = q_ref.shape[0]

    def tile(kb, vb, carry):
        lane_v = lax.broadcasted_iota(jnp.int32, vb.shape, 1)
        one = jnp.ones_like(vb)
        vbs = (jnp.where(lane_v < MLA_V, vb, one), jnp.where(lane_v >= MLA_V, vb, one))
        out = []
        for hh in range(2):
            m, acc = carry[hh]
            hs = slice(hh * HEAD_BLK, (hh + 1) * HEAD_BLK)
            s = _dot_nt(q_ref[:, hs], kb[:, hs])
            m_new = jnp.maximum(m, jnp.max(s, axis=-1, keepdims=True))
            alpha = jnp.exp2(m - m_new)
            p = jnp.exp2(s - m_new)
            acc = alpha * acc + _dot(p.astype(BF16), vbs[hh])
            out.append((m_new, acc))
        return tuple(out)

    init = tuple((jnp.full((tq, 1), NEG_INF, F32), jnp.zeros((tq, LANES), F32)) for _ in range(2))
    carry = tile(kc_ref[...], vc_ref[...], init)
    if n_chunks:
        def body(j, c):
            r = pl.ds(pl.multiple_of(j * tk, tk), tk)
            return tile(k_ref[r, :], v_ref[r, :], c)
        carry = lax.fori_loop(0, n_chunks, body, carry, unroll=True)
    lane = lax.broadcasted_iota(jnp.int32, (tq, LANES), 1)
    a0, a1 = carry[0][1], carry[1][1]
    o_ref[...] = jnp.where(lane < MLA_V, a0 / pltpu.roll(a0, MLA_V, 1), a1 / pltpu.roll(a1, MLA_V, 1)).astype(BF16)


def _mla(q, kc, vc, k=None, v=None):
    B, Tq, _ = q.shape
    C = kc.shape[1]
    tq = min(MLA_TQ, Tq)
    nh2 = MLA_HEADS // 2
    in_specs = [pl.BlockSpec((None, tq, 2 * HEAD_BLK), lambda b, h, i: (b, i, h)),
                pl.BlockSpec((None, C, 2 * HEAD_BLK), lambda b, h, i: (b, 0, h)),
                pl.BlockSpec((None, C, 2 * MLA_V), lambda b, h, i: (b, 0, h))]
    args = [q, kc, vc]
    n_chunks, tk = 0, 0
    if k is not None:
        T = k.shape[1]
        tk = min(MLA_TK, T)
        n_chunks = T // tk
        in_specs += [pl.BlockSpec((None, T, 2 * HEAD_BLK), lambda b, h, i: (b, 0, h)),
                     pl.BlockSpec((None, T, 2 * MLA_V), lambda b, h, i: (b, 0, h))]
        args += [k, v]
    return pl.pallas_call(
        functools.partial(_mla_kernel, tk=tk, n_chunks=n_chunks),
        grid=(B, nh2, Tq // tq),
        in_specs=in_specs,
        out_specs=pl.BlockSpec((None, tq, 2 * MLA_V), lambda b, h, i: (b, i, h)),
        out_shape=jax.ShapeDtypeStruct((B, Tq, _VW), BF16),
        compiler_params=_params(("arbitrary", "arbitrary", "arbitrary")),
        name="mla",
    )(*args)


def _gqa_kernel(sink_ref, q_ref, kc_ref, vc_ref, *rest, tq, band, seq, layer):
    if band:
        k_ref, v_ref, o_ref = rest
    else:
        (o_ref,) = rest
    half = GQA_HEAD_DIM
    lane = lax.broadcasted_iota(jnp.int32, (tq, LANES), 1)
    kc = kc_ref[...]
    vc = vc_ref[...]
    if band:
        nk = tq + 2 * WINDOW
        q0 = pl.program_id(1) * tq
        start = pl.multiple_of(jnp.clip(q0 - WINDOW, 0, seq - nk), LANES)
        kw = k_ref[pl.ds(start, nk), :]
        vw = v_ref[pl.ds(start, nk), :]
        r = lax.broadcasted_iota(jnp.int32, (2 * tq, nk), 0)
        qpos = q0 + jnp.where(r < tq, r, r - tq)
        kpos = start + lax.broadcasted_iota(jnp.int32, (2 * tq, nk), 1)
        valid = jnp.abs(qpos - kpos) <= WINDOW
    rows = lax.broadcasted_iota(jnp.int32, (2 * tq, 1), 0)
    for j in range(GQA_HEADS // 2):
        qb = q_ref[:, j * LANES:(j + 1) * LANES]
        zero = jnp.zeros_like(qb)
        qq = jnp.concatenate([jnp.where(lane < half, qb, zero), jnp.where(lane >= half, qb, zero)], axis=0)
        sink = jnp.where(rows < tq, sink_ref[layer, j], sink_ref[layer, GQA_HEADS // 2 + j])
        s_c = _dot_nt(qq, kc)
        m = jnp.maximum(sink, jnp.max(s_c, axis=-1, keepdims=True))
        if band:
            s_b = jnp.where(valid, _dot_nt(qq, kw), NEG_INF)
            m = jnp.maximum(m, jnp.max(s_b, axis=-1, keepdims=True))
        p_c = jnp.exp2(s_c - m)
        den = jnp.exp2(sink - m) + jnp.sum(p_c, axis=-1, keepdims=True)
        o = _dot(p_c.astype(BF16), vc)
        if band:
            p_b = jnp.exp2(s_b - m)
            den = den + jnp.sum(p_b, axis=-1, keepdims=True)
            o = o + _dot(p_b.astype(BF16), vw)
        o = o / den
        o_ref[:, j * LANES:(j + 1) * LANES] = jnp.where(lane < half, o[:tq], o[tq:]).astype(BF16)


def _gqa(sink, l, q, kc, vc, k=None, v=None):
    B, Tq, _ = q.shape
    C = kc.shape[1]
    band = k is not None
    tq = min(GQA_TQ, Tq)
    full = lambda n: pl.BlockSpec((None, n, _GKW), lambda b, i: (b, 0, 0))
    in_specs = [pl.BlockSpec(memory_space=pltpu.SMEM),
                pl.BlockSpec((None, tq, _GQW), lambda b, i: (b, i, 0)),
                full(C), full(C)]
    args = [sink, q, kc, vc]
    seq = 0
    if band:
        seq = k.shape[1]
        assert seq >= tq + 2 * WINDOW
        in_specs += [full(seq), full(seq)]
        args += [k, v]
    return pl.pallas_call(
        functools.partial(_gqa_kernel, tq=tq, band=band, seq=seq, layer=l),
        grid=(B, Tq // tq),
        in_specs=in_specs,
        out_specs=pl.BlockSpec((None, tq, _GQW), lambda b, i: (b, i, 0)),
        out_shape=jax.ShapeDtypeStruct((B, Tq, _GQW), BF16),
        compiler_params=_params(("arbitrary", "arbitrary")),
        name="gqa",
    )(*args)


def _gelu(y):
    return 0.5 * y * (1.0 + jnp.tanh(math.sqrt(2.0 / math.pi) * (y + 0.044715 * (y * y * y))))


def _ssm_kernel(zc_ref, zl_ref, we_ref, wy_ref, a_ref, yc_ref, yl_ref, e_ref, sf_ref, sr_ref,
                *, n_ctx, n_lat, batch):
    sw = PAIR * SSM_STATE
    n_all = n_ctx + n_lat
    segs = ((zc_ref, yc_ref, 0, n_ctx), (zl_ref, yl_ref, n_ctx, n_lat))

    def chunk_rows(c0, n, pp, b):
        return pl.ds(c0 * SUBLANES + pp * batch + b, n, stride=SUBLANES)

    def z_pair(z_ref, b, pp):
        return jnp.concatenate([z_ref[b, PAIR * pp], z_ref[b, PAIR * pp + 1]], axis=1)

    for z_ref, _, c0, n in segs:
        for pp in range(PAIR):
            for b in range(batch):
                e = _dot(z_pair(z_ref, b, pp), we_ref[pp])
                for k in range(4):
                    e_ref[k, chunk_rows(c0, n, pp, b), :] = e[:, k * sw:(k + 1) * sw]
    af_re, af_im, ar_re, ar_im = a_ref[0], a_ref[1], a_ref[2], a_ref[3]
    zero = jnp.zeros((SUBLANES, sw), F32)

    def rows_of(c):
        return pl.ds(pl.multiple_of(c * SUBLANES, SUBLANES), SUBLANES)

    def fwd(c, carry):
        s_re, s_im = carry
        r = rows_of(c)
        sf_ref[0, r, :] = s_re
        sf_ref[1, r, :] = s_im
        return (af_re * s_re - af_im * s_im + e_ref[0, r, :],
                af_re * s_im + af_im * s_re + e_ref[1, r, :])

    lax.fori_loop(0, n_all, fwd, (zero, zero))

    def rev_at(c, carry):
        s_re, s_im = carry
        r = rows_of(c)
        sr_ref[0, r, :] = s_re
        sr_ref[1, r, :] = s_im
        return (ar_re * s_re - ar_im * s_im + e_ref[2, r, :],
                ar_re * s_im + ar_im * s_re + e_ref[3, r, :])

    carry = lax.fori_loop(0, n_ctx, lambda i, c: rev_at(n_ctx - 1 - i, c), (zero, zero))
    lax.fori_loop(0, n_lat, lambda i, c: rev_at(n_all - 1 - i, c), carry)

    uw = PAIR * _ZW
    for z_ref, y_ref, c0, n in segs:
        for pp in range(PAIR):
            for b in range(batch):
                r = chunk_rows(c0, n, pp, b)
                s_f = jnp.concatenate([sf_ref[0, r, :], sf_ref[1, r, :]], axis=1).astype(BF16)
                s_r = jnp.concatenate([sr_ref[0, r, :], sr_ref[1, r, :]], axis=1).astype(BF16)
                y = (_dot(z_pair(z_ref, b, pp), wy_ref[pp, 0:uw, :])
                     + _dot(s_f, wy_ref[pp, uw:uw + 2 * sw, :])
                     + _dot(s_r, wy_ref[pp, uw + 2 * sw:uw + 4 * sw, :]))
                y = _gelu(y).astype(BF16)
                y_ref[b, PAIR * pp] = y[:, :_ZW]
                y_ref[b, PAIR * pp + 1] = y[:, _ZW:]


def _ssm(z_lat, z_ctx, we, wy, a, l):
    B, G, n_lat, _ = z_lat.shape
    n_ctx = z_ctx.shape[2]
    assert PAIR * B == SUBLANES, "scan rows pack (pair, batch) into one sublane tile"
    gstep = PAIR * PAIR
    rows = (n_ctx + n_lat) * SUBLANES
    sw2 = 2 * PAIR * SSM_STATE
    uw = PAIR * _ZW
    zspec = lambda n: pl.BlockSpec((B, gstep, n, _ZW), lambda g: (0, g, 0, 0))
    y_ctx, y_lat = pl.pallas_call(
        functools.partial(_ssm_kernel, n_ctx=n_ctx, n_lat=n_lat, batch=B),
        grid=(G // gstep,),
        in_specs=[zspec(n_ctx), zspec(n_lat),
                  pl.BlockSpec((None, PAIR, uw, 2 * sw2), lambda g: (l, g, 0, 0)),
                  pl.BlockSpec((None, PAIR, uw + 2 * sw2, uw), lambda g: (l, g, 0, 0)),
                  pl.BlockSpec((None, None, 4, SUBLANES, PAIR * SSM_STATE), lambda g: (l, g, 0, 0, 0))],
        out_specs=[zspec(n_ctx), zspec(n_lat)],
        out_shape=[jax.ShapeDtypeStruct(z_ctx.shape, BF16), jax.ShapeDtypeStruct(z_lat.shape, BF16)],
        scratch_shapes=[pltpu.VMEM((4, rows, PAIR * SSM_STATE), F32),
                        pltpu.VMEM((2, rows, PAIR * SSM_STATE), F32),
                        pltpu.VMEM((2, rows, PAIR * SSM_STATE), F32)],
        compiler_params=_params(("arbitrary",)),
        name="ssm",
    )(z_ctx, z_lat, we, wy, a)
    return y_lat, y_ctx


def _merge_kernel(h_ref, mod_ref, a_ref, yg_ref, g_ref, gate_ref, wo_ref, wglu_ref, wgo_ref, wout_ref, o_ref,
                  ys_ref, *, d_model):
    D = d_model
    b0 = _dot(a_ref[...], wo_ref[...])
    nr = ys_ref.shape[1] // CHUNK
    masks = _piece_masks(nr)
    for lt in range(SSM_WIDTH // LANES):
        for jh in range(_ZW // LANES):
            a = [yg_ref[lt * _PIECES + s, :, jh * LANES:(jh + 1) * LANES].astype(F32) for s in range(_PIECES)]
            b = _block_transpose(a, masks)
            for d in range(_PIECES):
                ys_ref[lt, pl.ds(jh * _PIECES + d, nr, stride=CHUNK), :] = b[d]
    ys = jnp.concatenate([ys_ref[lt] for lt in range(SSM_WIDTH // LANES)], axis=1)
    t = _dot(ys.astype(BF16), wglu_ref[...])
    b1 = t[:, :D] * _sigmoid(t[:, D:])
    b2 = _dot(g_ref[...], wgo_ref[...])
    gate = gate_ref[...].astype(F32)
    mix = gate[:, 0:D] * b0 + gate[:, D:2 * D] * b1 + gate[:, 2 * D:3 * D] * b2
    o_ref[...] = h_ref[...] + mod_ref[5:6, :] * _dot(mix.astype(BF16), wout_ref[...])


def _merge(h, mods, l, mrow, attn, yssm, gqa, gate, lw):
    B, R, D = h.shape
    tm = min(512, R)
    row = lambda w: pl.BlockSpec((None, tm, w), lambda b, i: (b, i, 0))
    return pl.pallas_call(
        functools.partial(_merge_kernel, d_model=D),
        grid=(B, R // tm),
        in_specs=[row(D),
                  _mod_spec(mods, l, mrow),
                  row(_VW),
                  pl.BlockSpec((None, SSM_GROUPS, tm // CHUNK, _ZW), lambda b, i: (b, 0, i, 0)),
                  row(_GQW), row(N_BRANCH * D),
                  _layer(lw["wo"], l), _layer(lw["wglu"], l), _layer(lw["wgo"], l), _layer(lw["wout"], l)],
        out_specs=row(D),
        out_shape=jax.ShapeDtypeStruct((B, R, D), F32),
        scratch_shapes=[pltpu.VMEM((SSM_WIDTH // LANES, tm, LANES), F32)],
        compiler_params=_params(("arbitrary", "arbitrary")),
        name="merge",
    )(h, mods, attn, yssm, gqa, gate, lw["wo"], lw["wglu"], lw["wgo"], lw["wout"])


def _rope_angles(T, n):
    t = np.arange(T)
    inv = ROPE_BASE ** (-np.arange(0, n, 2, dtype=np.float32) / n)
    out = []
    for pos in (t // GRID_W, t % GRID_W):
        ang = jnp.asarray(pos.astype(np.float32))[:, None] * jnp.asarray(inv)[None, :]
        out.append((jnp.cos(ang), jnp.sin(ang)))
    cos = jnp.concatenate([out[0][0], out[0][0], out[1][0], out[1][0]], axis=1)
    sin = jnp.concatenate([-out[0][1], out[0][1], -out[1][1], out[1][1]], axis=1)
    return cos, sin


def _tables(T, C):
    sc_m = (MLA_NOPE + MLA_ROPE) ** -0.5 * LOG2E
    sc_g = GQA_HEAD_DIM ** -0.5 * LOG2E
    cm, sm = _rope_angles(T, MLA_ROPE // 2)
    cg, sg = _rope_angles(T, GQA_HEAD_DIM // 2)
    one = lambda n, w: jnp.ones((n, w), F32)
    zero = lambda n, w: jnp.zeros((n, w), F32)
    pad = LANES - MLA_NOPE - MLA_ROPE

    def mla_tab(n, c, s):
        cos_q = jnp.concatenate([one(n, MLA_NOPE), c, zero(n, pad)], axis=1) * sc_m
        sin_q = jnp.concatenate([zero(n, MLA_NOPE), s, zero(n, pad)], axis=1) * sc_m
        kr_cos = jnp.concatenate([c, zero(n, LANES - MLA_ROPE)], axis=1)
        kr_sin = jnp.concatenate([s, zero(n, LANES - MLA_ROPE)], axis=1)
        return jnp.concatenate([cos_q, sin_q, kr_cos, kr_sin], axis=1)

    def gqa_tab(n, c, s):
        c2, s2 = jnp.concatenate([c, c], axis=1), jnp.concatenate([s, s], axis=1)
        return jnp.concatenate([c2 * sc_g, s2 * sc_g, c2, s2], axis=1)

    lat = (mla_tab(T, cm, sm), gqa_tab(T, cg, sg))
    ctx = (mla_tab(C, one(C, MLA_ROPE), zero(C, MLA_ROPE)),
           gqa_tab(C, one(C, GQA_HEAD_DIM), zero(C, GQA_HEAD_DIM)))
    return lat, ctx


def _place_matrix():
    e = np.zeros((LANES, _QW), np.float32)
    for h in range(MLA_HEADS):
        for j in range(MLA_ROPE):
            e[j, h * HEAD_BLK + MLA_NOPE + j] = 1.0
    return jnp.asarray(e, BF16)


def _layer_weights(w_in, q_norm, kv_norm, w_uq, w_ukv, w_o, w_glu, gqa_w_o, w_out):
    D = w_in.shape[0]
    offs = np.cumsum([MLA_Q_RANK, MLA_KV_RANK, MLA_ROPE, SSM_WIDTH, _GQW, _GKW, _GKW]).tolist()
    cq, ckv, kr, u, gq, gk, gv, gates = jnp.split(w_in, offs, axis=1)
    order = np.asarray(GQA_ORDER)
    gq_h = gq.reshape(D, GQA_HEADS, GQA_HEAD_DIM)
    kr_blk = jnp.concatenate([kr, jnp.zeros((D, LANES - MLA_ROPE), F32)], axis=1)
    w1 = jnp.concatenate([cq, gv, ckv, u, gq_h[:, order].reshape(D, _GQW), gk, kr_blk, gates], axis=1).astype(BF16)
    uq = w_uq.reshape(MLA_Q_RANK, MLA_HEADS, MLA_NOPE + MLA_ROPE)
    zpad = jnp.zeros((MLA_Q_RANK, MLA_HEADS, HEAD_BLK - MLA_NOPE - MLA_ROPE), F32)
    wq = jnp.concatenate([uq, zpad], axis=-1).reshape(MLA_Q_RANK, _QW).astype(BF16)
    ukv = w_ukv.reshape(MLA_KV_RANK, MLA_HEADS, MLA_NOPE + MLA_V)
    wk = jnp.concatenate([ukv[..., :MLA_NOPE], jnp.zeros((MLA_KV_RANK, MLA_HEADS, HEAD_BLK - MLA_NOPE), F32)],
                         axis=-1).reshape(MLA_KV_RANK, _QW)
    wv = ukv[..., MLA_NOPE:].reshape(MLA_KV_RANK, _VW)
    wkv = jnp.concatenate([wk, wv], axis=1).astype(BF16)
    wgo = gqa_w_o.reshape(GQA_HEADS, GQA_HEAD_DIM, D)[order].reshape(_GQW, D).astype(BF16)
    return dict(w1=w1, qn=q_norm.reshape(1, -1), kvn=kv_norm.reshape(1, -1), wq=wq, wkv=wkv,
                wo=w_o.astype(BF16), wglu=w_glu.astype(BF16), wgo=wgo, wout=w_out.astype(BF16))


def _ssm_weights(lam_re, lam_im, log_dt, b_re, b_im, c_re, c_im, d_skip):
    G, P, M, Lc = SSM_GROUPS, SSM_STATE, SSM_GROUP, CHUNK
    dt = jnp.exp(log_dt)[..., None]
    kk = jnp.arange(Lc + 1, dtype=F32)[:, None, None, None]
    mag = jnp.exp(lam_re[None] * dt[None] * kk)
    pw_re, pw_im = mag * jnp.cos(lam_im[None] * dt[None] * kk), mag * jnp.sin(lam_im[None] * dt[None] * kk)
    a_re, a_im = pw_re[1], pw_im[1]
    den = lam_re * lam_re + lam_im * lam_im
    w_re = ((a_re - 1) * lam_re + a_im * lam_im) / den
    w_im = (a_im * lam_re - (a_re - 1) * lam_im) / den
    bb_re = w_re[..., None] * b_re - w_im[..., None] * b_im
    bb_im = w_re[..., None] * b_im + w_im[..., None] * b_re
    ca_re = c_re[:, None] * jnp.moveaxis(pw_re, 0, 1)[:, :, :, None, :] - c_im[:, None] * jnp.moveaxis(pw_im, 0, 1)[:, :, :, None, :]
    ca_im = c_re[:, None] * jnp.moveaxis(pw_im, 0, 1)[:, :, :, None, :] + c_im[:, None] * jnp.moveaxis(pw_re, 0, 1)[:, :, :, None, :]
    cb = jnp.sum(ca_re[..., None] * bb_re[:, None, :, None] - ca_im[..., None] * bb_im[:, None, :, None], axis=-2)
    lags_f = jnp.transpose(cb[0, :Lc], (1, 3, 0, 2)).reshape(G, M, Lc * M)
    lags_r = jnp.transpose(cb[1, :Lc][::-1], (1, 3, 0, 2)).reshape(G, M, Lc * M)
    halo = jnp.zeros((G, M, (Lc - 1) * M), F32)
    ext = jnp.concatenate([halo, lags_f], axis=-1) + jnp.concatenate([lags_r, halo], axis=-1)
    kmat = jnp.stack([ext[:, :, (Lc - 1 - i) * M:(Lc - 1 - i) * M + Lc * M] for i in range(Lc)], axis=1)
    skip = jnp.tile(d_skip.reshape(G, M), (1, Lc))
    kmat = kmat.reshape(G, Lc * M, Lc * M) + jnp.eye(Lc * M, dtype=F32)[None] * skip[:, None, :]
    idx_f = np.arange(Lc)[::-1].copy()
    idx_r = np.arange(Lc)

    def drive(d, idx):
        p_re, p_im = pw_re[idx, d], pw_im[idx, d]
        e_re = p_re[..., None] * bb_re[d][None] - p_im[..., None] * bb_im[d][None]
        e_im = p_re[..., None] * bb_im[d][None] + p_im[..., None] * bb_re[d][None]
        to = lambda t: jnp.transpose(t, (1, 0, 3, 2)).reshape(G, Lc * M, P)
        return to(e_re), to(e_im)

    ef_re, ef_im = drive(0, idx_f)
    er_re, er_im = drive(1, idx_r)

    def readout(d, idx):
        q_re, q_im = ca_re[d][idx], ca_im[d][idx]
        to = lambda t: jnp.transpose(t, (1, 3, 0, 2)).reshape(G, P, Lc * M)
        return to(q_re), to(-q_im)

    qf_re, qf_im = readout(0, np.arange(1, Lc + 1))
    qr_re, qr_im = readout(1, Lc - np.arange(Lc))
    npair = G // PAIR

    def pair_diag(t):
        R, Cn = t.shape[1:]
        t = t.reshape(npair, PAIR, R, Cn)
        z = jnp.zeros((npair, R, Cn), F32)
        rows = [jnp.concatenate([t[:, p] if q == p else z for q in range(PAIR)], axis=-1) for p in range(PAIR)]
        return jnp.concatenate(rows, axis=1)

    we = jnp.concatenate([pair_diag(ef_re), pair_diag(ef_im), pair_diag(er_re), pair_diag(er_im)], axis=2)
    wy = jnp.concatenate([pair_diag(kmat), pair_diag(qf_re), pair_diag(qf_im), pair_diag(qr_re), pair_diag(qr_im)], axis=1)
    sw = PAIR * P
    nstep = npair // PAIR

    def per_row(t):
        t = t.reshape(nstep, PAIR, 1, sw)
        return jnp.broadcast_to(t, (nstep, PAIR, SUBLANES // PAIR, sw)).reshape(nstep, SUBLANES, sw)

    a_step = jnp.stack([per_row(pw_re[Lc, 0]), per_row(pw_im[Lc, 0]), per_row(pw_re[Lc, 1]), per_row(pw_im[Lc, 1])],
                       axis=1)
    return we.astype(BF16), wy.astype(BF16), a_step


def kernel(x, c, ctx, c_ctx, ada_w, ada_b, norm_ffn1, norm_mix, norm_ffn2, ffn1_w13, ffn1_w2, ffn2_w13, ffn2_w2, w_in, mla_q_norm, mla_kv_norm, mla_w_uq, mla_w_ukv, mla_w_o, ssm_lambda_re, ssm_lambda_im, ssm_log_dt, ssm_b_re, ssm_b_im, ssm_c_re, ssm_c_im, ssm_d, ssm_w_glu, gqa_sink, gqa_w_o, w_out, final_norm):
    B, T, D = x.shape
    C = ctx.shape[1]
    depth = ada_w.shape[0]
    cc = jnp.concatenate([c, c_ctx[None], jnp.zeros((SUBLANES - B - 1, D), F32)], axis=0)
    mods = _ada(cc, ada_w, ada_b).reshape(depth, SUBLANES, N_MOD, D)
    (tabm, tabg), (tabm_c, tabg_c) = _tables(T, C)
    sinks = gqa_sink * LOG2E
    lw = jax.vmap(_layer_weights)(w_in, mla_q_norm, mla_kv_norm, mla_w_uq, mla_w_ukv, mla_w_o, ssm_w_glu, gqa_w_o,
                                  w_out)
    we, wy, a_step = jax.vmap(_ssm_weights)(ssm_lambda_re, ssm_lambda_im, ssm_log_dt, ssm_b_re, ssm_b_im,
                                            ssm_c_re, ssm_c_im, ssm_d)
    e_mat = _place_matrix()
    ffn1 = (norm_ffn1[:, None, :], ffn1_w13.astype(BF16), ffn1_w2.astype(BF16))
    ffn2 = (norm_ffn2[:, None, :], ffn2_w13.astype(BF16), ffn2_w2.astype(BF16))
    g_mix = norm_mix[:, None, :]
    lat, cx = None, B
    h, hc = x, ctx
    for l in range(depth):
        ctx_out = l < depth - 1
        last = l == depth - 1
        h = _ffn(h, mods, l, lat, *ffn1, 0)
        hc = _ffn(hc, mods, l, cx, *ffn1, 0)
        q, k, v, z, gq, gk, gv, gate = _inproj(h, mods, l, lat, g_mix, lw, e_mat, tabm, tabg)
        q_c, k_c, v_c, z_c, gq_c, gk_c, gv_c, gate_c = _inproj(hc, mods, l, cx, g_mix, lw, e_mat, tabm_c, tabg_c)
        attn = _mla(q, k_c, v_c, k, v)
        y_lat, y_ctx = _ssm(z, z_c, we, wy, a_step, l)
        gqa = _gqa(sinks, l, gq, gk_c, gv_c, gk, gv)
        h = _merge(h, mods, l, lat, attn, y_lat, gqa, gate, lw)
        h = _ffn(h, mods, l, lat, *ffn2, 6, final_norm if last else None)
        if ctx_out:
            attn_c = _mla(q_c, k_c, v_c)
            gqa_c = _gqa(sinks, l, gq_c, gk_c, gv_c)
            hc = _merge(hc, mods, l, cx, attn_c, y_ctx, gqa_c, gate_c, lw)
            hc = _ffn(hc, mods, l, cx, *ffn2, 6)
    return h
```

```python
import functools
import math

import numpy as np
import jax
import jax.numpy as jnp
from jax import lax
from jax.experimental import pallas as pl
from jax.experimental.pallas import tpu as pltpu

F32 = jnp.float32
BF16 = jnp.bfloat16

GRID_W = 64
MLA_HEADS = 8
MLA_NOPE = 64
MLA_ROPE = 32
MLA_V = 64
MLA_Q_RANK = 384
MLA_KV_RANK = 256
SSM_WIDTH = 512
SSM_GROUP = 16
SSM_GROUPS = SSM_WIDTH // SSM_GROUP
SSM_STATE = 64
GQA_HEADS = 8
GQA_KV_HEADS = 2
GQA_HEAD_DIM = 64
WINDOW = 128
N_BRANCH = 3
N_MOD = 9
ROPE_BASE = 10000.0
EPS = 1e-6
NEG_INF = -1e30
LOG2E = math.log2(math.e)

LANES = 128
SUBLANES = 8
VMEM_LIMIT = 56 * 1024 * 1024
CHUNK = 16
PAIR = 2
HEAD_BLK = LANES
MXU_COLS = 256
FFN_TM = 1024
GQA_TQ = 256
MLA_TQ = 1024
MLA_TK = 2048
GQA_ORDER = (0, 4, 1, 5, 2, 6, 3, 7)


def _dot(a, b):
    return jnp.dot(a, b, preferred_element_type=F32)


def _dot_nt(a, b):
    return lax.dot_general(a, b, (((1,), (1,)), ((), ())), preferred_element_type=F32)


def _rms(x, g):
    return x * lax.rsqrt(jnp.mean(x * x, axis=-1, keepdims=True) + EPS) * g


def _sigmoid(x):
    return 0.5 * jnp.tanh(0.5 * x) + 0.5


def _params(sem, vmem=VMEM_LIMIT):
    return pltpu.CompilerParams(dimension_semantics=sem, vmem_limit_bytes=vmem)


def _resident(shape):
    nd = len(shape)
    return pl.BlockSpec(shape, lambda *_: (0,) * nd, pipeline_mode=pl.Buffered(1))


def _layer(arr, l):
    nd = arr.ndim
    return pl.BlockSpec((None,) + arr.shape[1:], lambda *_: (l,) + (0,) * (nd - 1), pipeline_mode=pl.Buffered(1))


def _mod_spec(mods, l, mrow):
    blk = (None, None) + mods.shape[2:]
    if mrow is None:
        return pl.BlockSpec(blk, lambda b, i: (l, b, 0, 0))
    return pl.BlockSpec(blk, lambda b, i: (l, mrow, 0, 0))


def _ada_kernel(c_ref, w_ref, b_ref, o_ref):
    c = c_ref[...]
    s = c * _sigmoid(c)
    w = w_ref[...]
    s_hi = s.astype(BF16)
    s_lo = (s - s_hi.astype(F32)).astype(BF16)
    w_hi = w.astype(BF16)
    w_lo = (w - w_hi.astype(F32)).astype(BF16)
    o_ref[...] = _dot(s_hi, w_hi) + _dot(s_hi, w_lo) + _dot(s_lo, w_hi) + b_ref[...]


def _ada(cc, ada_w, ada_b):
    L, D, N = ada_w.shape
    tn = 1152 if N % 1152 == 0 else N
    R = cc.shape[0]
    return pl.pallas_call(
        _ada_kernel,
        grid=(L, N // tn),
        in_specs=[pl.BlockSpec((R, D), lambda l, j: (0, 0)),
                  pl.BlockSpec((None, D, tn), lambda l, j: (l, 0, j)),
                  pl.BlockSpec((None, 1, tn), lambda l, j: (l, 0, j))],
        out_specs=pl.BlockSpec((None, R, tn), lambda l, j: (l, 0, j)),
        out_shape=jax.ShapeDtypeStruct((L, R, N), F32),
        compiler_params=_params(("arbitrary", "arbitrary")),
        name="ada",
    )(cc, ada_w, ada_b.reshape(L, 1, N))


def _ffn_kernel(h_ref, mod_ref, g_ref, w13_ref, w2_ref, *rest, k0, ff, tf, final):
    if final:
        fg_ref, o_ref = rest
    else:
        (o_ref,) = rest
    x = h_ref[...]
    mod = mod_ref[...]
    xn = _rms(x, g_ref[...]) * (1.0 + mod[k0 + 1:k0 + 2]) + mod[k0:k0 + 1]
    xb = xn.astype(BF16)
    acc = None
    for f0, f1 in tf:
        a = _dot(xb, w13_ref[:, f0:f1])
        b = _dot(xb, w13_ref[:, ff + f0:ff + f1])
        act = (a * _sigmoid(a) * b).astype(BF16)
        part = _dot(act, w2_ref[f0:f1, :])
        acc = part if acc is None else acc + part
    out = x + (0.5 * mod[k0 + 2:k0 + 3]) * acc
    if final:
        out = _rms(out, fg_ref[...])
    o_ref[...] = out


def _ffn(h, mods, l, mrow, g, w13, w2, k0, final_g=None):
    B, R, D = h.shape
    ff = w2.shape[1]
    tm = min(FFN_TM, R)
    cut = (ff // MXU_COLS // 2) * MXU_COLS
    tf = ((0, cut), (cut, ff)) if 0 < cut < ff else ((0, ff),)
    final = final_g is not None
    in_specs = [pl.BlockSpec((None, tm, D), lambda b, i: (b, i, 0)),
                _mod_spec(mods, l, mrow),
                _layer(g, l), _layer(w13, l), _layer(w2, l)]
    args = [h, mods, g, w13, w2]
    if final:
        in_specs.append(_resident((1, D)))
        args.append(final_g.reshape(1, D))
    return pl.pallas_call(
        functools.partial(_ffn_kernel, k0=k0, ff=ff, tf=tf, final=final),
        grid=(B, R // tm),
        in_specs=in_specs,
        out_specs=pl.BlockSpec((None, tm, D), lambda b, i: (b, i, 0)),
        out_shape=jax.ShapeDtypeStruct((B, R, D), F32),
        compiler_params=_params(("arbitrary", "arbitrary")),
        name="ffn",
    )(*args)


_O_CQ = 0
_O_GV = _O_CQ + MLA_Q_RANK
_O_CKV = _O_GV + GQA_KV_HEADS * GQA_HEAD_DIM
_O_U = _O_CKV + MLA_KV_RANK
_O_GQ = _O_U + SSM_WIDTH
_O_GK = _O_GQ + GQA_HEADS * GQA_HEAD_DIM
_O_KR = _O_GK + GQA_KV_HEADS * GQA_HEAD_DIM
_O_GATE = _O_KR + LANES
_QW = MLA_HEADS * HEAD_BLK
_VW = MLA_HEADS * MLA_V
_GQW = GQA_HEADS * GQA_HEAD_DIM
_GKW = GQA_KV_HEADS * GQA_HEAD_DIM


_PIECES = LANES // SSM_GROUP
_ZW = CHUNK * SSM_GROUP


def _piece_masks(rows):
    piece = lax.broadcasted_iota(jnp.int32, (rows, LANES), 1) // SSM_GROUP
    return {b: (piece & b) != 0 for b in (4, 2, 1)}


def _block_transpose(a, masks):
    a = list(a)
    for b in (4, 2, 1):
        hi = masks[b]
        new = list(a)
        for s in range(_PIECES):
            if s & b == 0:
                t = s | b
                new[s] = jnp.where(hi, pltpu.roll(a[t], b * SSM_GROUP, 1), a[s])
                new[t] = jnp.where(hi, a[t], pltpu.roll(a[s], LANES - b * SSM_GROUP, 1))
        a = new
    return a


def _inproj_kernel(h_ref, mod_ref, g_ref, w1_ref, qn_ref, kvn_ref, wq_ref, wkv_ref, e_ref, tabm_ref, tabg_ref,
                   q_ref, k_ref, v_ref, z_ref, gq_ref, gk_ref, gv_ref, gate_ref, us_ref, *, d_model):
    x = h_ref[...]
    mod = mod_ref[...]
    xb = (_rms(x, g_ref[...]) * (1.0 + mod[4:5]) + mod[3:4]).astype(BF16)

    def proj(o, n):
        return _dot(xb, w1_ref[:, o:o + n])

    lane = lax.broadcasted_iota(jnp.int32, (x.shape[0], LANES), 1)

    def rope(t, n_half, cos, sin_signed):
        first = (lane & (2 * n_half - 1)) < n_half
        partner = jnp.where(first, pltpu.roll(t, LANES - n_half, 1), pltpu.roll(t, n_half, 1))
        return t * cos + partner * sin_signed

    cq_gv = proj(_O_CQ, MLA_Q_RANK + _GKW)
    gv_ref[...] = cq_gv[:, MLA_Q_RANK:].astype(BF16)
    cqn = _rms(cq_gv[:, :MLA_Q_RANK], qn_ref[...]).astype(BF16)
    qq = _dot(cqn, wq_ref[...])
    cos_q = tabm_ref[:, 0:LANES]
    sin_q = tabm_ref[:, LANES:2 * LANES]
    for h in range(MLA_HEADS):
        hs = slice(h * HEAD_BLK, (h + 1) * HEAD_BLK)
        q_ref[:, hs] = rope(qq[:, hs], MLA_ROPE // 4, cos_q, sin_q).astype(BF16)
    ckvn = _rms(proj(_O_CKV, MLA_KV_RANK), kvn_ref[...]).astype(BF16)
    gk_kr = proj(_O_GK, _GKW + LANES)
    kr = rope(gk_kr[:, _GKW:], MLA_ROPE // 4, tabm_ref[:, 2 * LANES:3 * LANES], tabm_ref[:, 3 * LANES:4 * LANES])
    k_ref[...] = (_dot(ckvn, wkv_ref[:, 0:_QW]) + _dot(kr.astype(BF16), e_ref[...])).astype(BF16)
    v_ref[...] = _dot(ckvn, wkv_ref[:, _QW:_QW + _VW]).astype(BF16)
    u = proj(_O_U, SSM_WIDTH)
    nr = us_ref.shape[1] // CHUNK
    masks = _piece_masks(nr)
    for lt in range(SSM_WIDTH // LANES):
        us_ref[lt] = u[:, lt * LANES:(lt + 1) * LANES]
        for jh in range(_ZW // LANES):
            a = [us_ref[lt, pl.ds(jh * _PIECES + s, nr, stride=CHUNK), :] for s in range(_PIECES)]
            b = _block_transpose(a, masks)
            for d in range(_PIECES):
                z_ref[lt * _PIECES + d, :, jh * LANES:(jh + 1) * LANES] = b[d].astype(BF16)
    cos_gq = tabg_ref[:, 0:LANES]
    sin_gq = tabg_ref[:, LANES:2 * LANES]
    cos_gk = tabg_ref[:, 2 * LANES:3 * LANES]
    sin_gk = tabg_ref[:, 3 * LANES:4 * LANES]
    gq = proj(_O_GQ, _GQW)
    for j in range(_GQW // LANES):
        sl = slice(j * LANES, (j + 1) * LANES)
        gq_ref[:, sl] = rope(gq[:, sl], GQA_HEAD_DIM // 4, cos_gq, sin_gq).astype(BF16)
    gk_ref[...] = rope(gk_kr[:, :_GKW], GQA_HEAD_DIM // 4, cos_gk, sin_gk).astype(BF16)
    for k in range(N_BRANCH):
        gate_ref[:, k * d_model:(k + 1) * d_model] = _sigmoid(proj(_O_GATE + k * d_model, d_model)).astype(BF16)


def _inproj(h, mods, l, mrow, g, lw, e_mat, tabm, tabg):
    B, R, D = h.shape
    tm = min(512, R)
    widths = (_QW, _QW, _VW, None, _GQW, _GKW, _GKW, N_BRANCH * D)
    row = lambda w: pl.BlockSpec((None, tm, w), lambda b, i: (b, i, 0))
    zspec = pl.BlockSpec((None, SSM_GROUPS, tm // CHUNK, _ZW), lambda b, i: (b, 0, i, 0))
    zshape = jax.ShapeDtypeStruct((B, SSM_GROUPS, R // CHUNK, _ZW), BF16)
    return pl.pallas_call(
        functools.partial(_inproj_kernel, d_model=D),
        grid=(B, R // tm),
        in_specs=[row(D),
                  _mod_spec(mods, l, mrow),
                  _layer(g, l), _layer(lw["w1"], l), _layer(lw["qn"], l), _layer(lw["kvn"], l),
                  _layer(lw["wq"], l), _layer(lw["wkv"], l),
                  _resident(e_mat.shape),
                  pl.BlockSpec((tm, 4 * LANES), lambda b, i: (i, 0)),
                  pl.BlockSpec((tm, 4 * LANES), lambda b, i: (i, 0))],
        out_specs=[zspec if w is None else row(w) for w in widths],
        out_shape=[zshape if w is None else jax.ShapeDtypeStruct((B, R, w), BF16) for w in widths],
        scratch_shapes=[pltpu.VMEM((SSM_WIDTH // LANES, tm, LANES), F32)],
        compiler_params=_params(("arbitrary", "arbitrary")),
        name="inproj",
    )(h, mods, g, lw["w1"], lw["qn"], lw["kvn"], lw["wq"], lw["wkv"], e_mat, tabm, tabg)


def _mla_kernel(q_ref, kc_ref, vc_ref, *rest, tk, n_chunks):
    if n_chunks:
        k_ref, v_ref, o_ref = rest
    else:
        (o_ref,) = rest
    tq = q_ref.shape[0]

    def tile(kb, vb, carry):
        lane_v = lax.broadcasted_iota(jnp.int32, vb.shape, 1)
        one = jnp.ones_like(vb)
        vbs = (jnp.where(lane_v < MLA_V, vb, one), jnp.where(lane_v >= MLA_V, vb, one))
        out = []
        for hh in range(2):
            m, acc = carry[hh]
            hs = slice(hh * HEAD_BLK, (hh + 1) * HEAD_BLK)
            s = _dot_nt(q_ref[:, hs], kb[:, hs])
            m_new = jnp.maximum(m, jnp.max(s, axis=-1, keepdims=True))
            alpha = jnp.exp2(m - m_new)
            p = jnp.exp2(s - m_new)
            acc = alpha * acc + _dot(p.astype(BF16), vbs[hh])
            out.append((m_new, acc))
        return tuple(out)

    init = tuple((jnp.full((tq, 1), NEG_INF, F32), jnp.zeros((tq, LANES), F32)) for _ in range(2))
    carry = tile(kc_ref[...], vc_ref[...], init)
    if n_chunks:
        def body(j, c):
            r = pl.ds(pl.multiple_of(j * tk, tk), tk)
            return tile(k_ref[r, :], v_ref[r, :], c)
        carry = lax.fori_loop(0, n_chunks, body, carry, unroll=True)
    lane = lax.broadcasted_iota(jnp.int32, (tq, LANES), 1)
    a0, a1 = carry[0][1], carry[1][1]
    o_ref[...] = jnp.where(lane < MLA_V, a0 / pltpu.roll(a0, MLA_V, 1), a1 / pltpu.roll(a1, MLA_V, 1)).astype(BF16)


def _mla(q, kc, vc, k=None, v=None):
    B, Tq, _ = q.shape
    C = kc.shape[1]
    tq = min(MLA_TQ, Tq)
    nh2 = MLA_HEADS // 2
    in_specs = [pl.BlockSpec((None, tq, 2 * HEAD_BLK), lambda b, h, i: (b, i, h)),
                pl.BlockSpec((None, C, 2 * HEAD_BLK), lambda b, h, i: (b, 0, h)),
                pl.BlockSpec((None, C, 2 * MLA_V), lambda b, h, i: (b, 0, h))]
    args = [q, kc, vc]
    n_chunks, tk = 0, 0
    if k is not None:
        T = k.shape[1]
        tk = min(MLA_TK, T)
        n_chunks = T // tk
        in_specs += [pl.BlockSpec((None, T, 2 * HEAD_BLK), lambda b, h, i: (b, 0, h)),
                     pl.BlockSpec((None, T, 2 * MLA_V), lambda b, h, i: (b, 0, h))]
        args += [k, v]
    return pl.pallas_call(
        functools.partial(_mla_kernel, tk=tk, n_chunks=n_chunks),
        grid=(B, nh2, Tq // tq),
        in_specs=in_specs,
        out_specs=pl.BlockSpec((None, tq, 2 * MLA_V), lambda b, h, i: (b, i, h)),
        out_shape=jax.ShapeDtypeStruct((B, Tq, _VW), BF16),
        compiler_params=_params(("arbitrary", "arbitrary", "arbitrary")),
        name="mla",
    )(*args)


def _gqa_kernel(sink_ref, q_ref, kc_ref, vc_ref, *rest, tq, band, seq, layer):
    if band:
        k_ref, v_ref, o_ref = rest
    else:
        (o_ref,) = rest
    half = GQA_HEAD_DIM
    lane = lax.broadcasted_iota(jnp.int32, (tq, LANES), 1)
    kc = kc_ref[...]
    vc = vc_ref[...]
    if band:
        nk = tq + 2 * WINDOW
        q0 = pl.program_id(1) * tq
        start = pl.multiple_of(jnp.clip(q0 - WINDOW, 0, seq - nk), LANES)
        kw = k_ref[pl.ds(start, nk), :]
        vw = v_ref[pl.ds(start, nk), :]
        r = lax.broadcasted_iota(jnp.int32, (2 * tq, nk), 0)
        qpos = q0 + jnp.where(r < tq, r, r - tq)
        kpos = start + lax.broadcasted_iota(jnp.int32, (2 * tq, nk), 1)
        valid = jnp.abs(qpos - kpos) <= WINDOW
    rows = lax.broadcasted_iota(jnp.int32, (2 * tq, 1), 0)
    for j in range(GQA_HEADS // 2):
        qb = q_ref[:, j * LANES:(j + 1) * LANES]
        zero = jnp.zeros_like(qb)
        qq = jnp.concatenate([jnp.where(lane < half, qb, zero), jnp.where(lane >= half, qb, zero)], axis=0)
        sink = jnp.where(rows < tq, sink_ref[layer, j], sink_ref[layer, GQA_HEADS // 2 + j])
        s_c = _dot_nt(qq, kc)
        m = jnp.maximum(sink, jnp.max(s_c, axis=-1, keepdims=True))
        if band:
            s_b = jnp.where(valid, _dot_nt(qq, kw), NEG_INF)
            m = jnp.maximum(m, jnp.max(s_b, axis=-1, keepdims=True))
        p_c = jnp.exp2(s_c - m)
        den = jnp.exp2(sink - m) + jnp.sum(p_c, axis=-1, keepdims=True)
        o = _dot(p_c.astype(BF16), vc)
        if band:
            p_b = jnp.exp2(s_b - m)
            den = den + jnp.sum(p_b, axis=-1, keepdims=True)
            o = o + _dot(p_b.astype(BF16), vw)
        o = o / den
        o_ref[:, j * LANES:(j + 1) * LANES] = jnp.where(lane < half, o[:tq], o[tq:]).astype(BF16)


def _gqa(sink, l, q, kc, vc, k=None, v=None):
    B, Tq, _ = q.shape
    C = kc.shape[1]
    band = k is not None
    tq = min(GQA_TQ, Tq)
    full = lambda n: pl.BlockSpec((None, n, _GKW), lambda b, i: (b, 0, 0))
    in_specs = [pl.BlockSpec(memory_space=pltpu.SMEM),
                pl.BlockSpec((None, tq, _GQW), lambda b, i: (b, i, 0)),
                full(C), full(C)]
    args = [sink, q, kc, vc]
    seq = 0
    if band:
        seq = k.shape[1]
        assert seq >= tq + 2 * WINDOW
        in_specs += [full(seq), full(seq)]
        args += [k, v]
    return pl.pallas_call(
        functools.partial(_gqa_kernel, tq=tq, band=band, seq=seq, layer=l),
        grid=(B, Tq // tq),
        in_specs=in_specs,
        out_specs=pl.BlockSpec((None, tq, _GQW), lambda b, i: (b, i, 0)),
        out_shape=jax.ShapeDtypeStruct((B, Tq, _GQW), BF16),
        compiler_params=_params(("arbitrary", "arbitrary")),
        name="gqa",
    )(*args)


def _gelu(y):
    return 0.5 * y * (1.0 + jnp.tanh(math.sqrt(2.0 / math.pi) * (y + 0.044715 * (y * y * y))))


def _ssm_kernel(zc_ref, zl_ref, we_ref, wy_ref, a_ref, yc_ref, yl_ref, e_ref, sf_ref, sr_ref,
                *, n_ctx, n_lat, batch):
    sw = PAIR * SSM_STATE
    n_all = n_ctx + n_lat
    segs = ((zc_ref, yc_ref, 0, n_ctx), (zl_ref, yl_ref, n_ctx, n_lat))

    def chunk_rows(c0, n, pp, b):
        return pl.ds(c0 * SUBLANES + pp * batch + b, n, stride=SUBLANES)

    def z_pair(z_ref, b, pp):
        return jnp.concatenate([z_ref[b, PAIR * pp], z_ref[b, PAIR * pp + 1]], axis=1)

    for z_ref, _, c0, n in segs:
        for pp in range(PAIR):
            for b in range(batch):
                e = _dot(z_pair(z_ref, b, pp), we_ref[pp])
                for k in range(4):
                    e_ref[k, chunk_rows(c0, n, pp, b), :] = e[:, k * sw:(k + 1) * sw]
    af_re, af_im, ar_re, ar_im = a_ref[0], a_ref[1], a_ref[2], a_ref[3]
    zero = jnp.zeros((SUBLANES, sw), F32)

    def rows_of(c):
        return pl.ds(pl.multiple_of(c * SUBLANES, SUBLANES), SUBLANES)

    def fwd(c, carry):
        s_re, s_im = carry
        r = rows_of(c)
        sf_ref[0, r, :] = s_re
        sf_ref[1, r, :] = s_im
        return (af_re * s_re - af_im * s_im + e_ref[0, r, :],
                af_re * s_im + af_im * s_re + e_ref[1, r, :])

    lax.fori_loop(0, n_all, fwd, (zero, zero))

    def rev_at(c, carry):
        s_re, s_im = carry
        r = rows_of(c)
        sr_ref[0, r, :] = s_re
        sr_ref[1, r, :] = s_im
        return (ar_re * s_re - ar_im * s_im + e_ref[2, r, :],
                ar_re * s_im + ar_im * s_re + e_ref[3, r, :])

    carry = lax.fori_loop(0, n_ctx, lambda i, c: rev_at(n_ctx - 1 - i, c), (zero, zero))
    lax.fori_loop(0, n_lat, lambda i, c: rev_at(n_all - 1 - i, c), carry)

    uw = PAIR * _ZW
    for z_ref, y_ref, c0, n in segs:
        for pp in range(PAIR):
            for b in range(batch):
                r = chunk_rows(c0, n, pp, b)
                s_f = jnp.concatenate([sf_ref[0, r, :], sf_ref[1, r, :]], axis=1).astype(BF16)
                s_r = jnp.concatenate([sr_ref[0, r, :], sr_ref[1, r, :]], axis=1).astype(BF16)
                y = (_dot(z_pair(z_ref, b, pp), wy_ref[pp, 0:uw, :])
                     + _dot(s_f, wy_ref[pp, uw:uw + 2 * sw, :])
                     + _dot(s_r, wy_ref[pp, uw + 2 * sw:uw + 4 * sw, :]))
                y = _gelu(y).astype(BF16)
                y_ref[b, PAIR * pp] = y[:, :_ZW]
                y_ref[b, PAIR * pp + 1] = y[:, _ZW:]


def _ssm(z_lat, z_ctx, we, wy, a, l):
    B, G, n_lat, _ = z_lat.shape
    n_ctx = z_ctx.shape[2]
    assert PAIR * B == SUBLANES, "scan rows pack (pair, batch) into one sublane tile"
    gstep = PAIR * PAIR
    rows = (n_ctx + n_lat) * SUBLANES
    sw2 = 2 * PAIR * SSM_STATE
    uw = PAIR * _ZW
    zspec = lambda n: pl.BlockSpec((B, gstep, n, _ZW), lambda g: (0, g, 0, 0))
    y_ctx, y_lat = pl.pallas_call(
        functools.partial(_ssm_kernel, n_ctx=n_ctx, n_lat=n_lat, batch=B),
        grid=(G // gstep,),
        in_specs=[zspec(n_ctx), zspec(n_lat),
                  pl.BlockSpec((None, PAIR, uw, 2 * sw2), lambda g: (l, g, 0, 0)),
                  pl.BlockSpec((None, PAIR, uw + 2 * sw2, uw), lambda g: (l, g, 0, 0)),
                  pl.BlockSpec((None, None, 4, SUBLANES, PAIR * SSM_STATE), lambda g: (l, g, 0, 0, 0))],
        out_specs=[zspec(n_ctx), zspec(n_lat)],
        out_shape=[jax.ShapeDtypeStruct(z_ctx.shape, BF16), jax.ShapeDtypeStruct(z_lat.shape, BF16)],
        scratch_shapes=[pltpu.VMEM((4, rows, PAIR * SSM_STATE), F32),
                        pltpu.VMEM((2, rows, PAIR * SSM_STATE), F32),
                        pltpu.VMEM((2, rows, PAIR * SSM_STATE), F32)],
        compiler_params=_params(("arbitrary",)),
        name="ssm",
    )(z_ctx, z_lat, we, wy, a)
    return y_lat, y_ctx


def _merge_kernel(h_ref, mod_ref, a_ref, yg_ref, g_ref, gate_ref, wo_ref, wglu_ref, wgo_ref, wout_ref, o_ref,
                  ys_ref, *, d_model):
    D = d_model
    b0 = _dot(a_ref[...], wo_ref[...])
    nr = ys_ref.shape[1] // CHUNK
    masks = _piece_masks(nr)
    for lt in range(SSM_WIDTH // LANES):
        for jh in range(_ZW // LANES):
            a = [yg_ref[lt * _PIECES + s, :, jh * LANES:(jh + 1) * LANES].astype(F32) for s in range(_PIECES)]
            b = _block_transpose(a, masks)
            for d in range(_PIECES):
                ys_ref[lt, pl.ds(jh * _PIECES + d, nr, stride=CHUNK), :] = b[d]
    ys = jnp.concatenate([ys_ref[lt] for lt in range(SSM_WIDTH // LANES)], axis=1)
    t = _dot(ys.astype(BF16), wglu_ref[...])
    b1 = t[:, :D] * _sigmoid(t[:, D:])
    b2 = _dot(g_ref[...], wgo_ref[...])
    mix = (gate_ref[:, 0:D].astype(F32) * b0 + gate_ref[:, D:2 * D].astype(F32) * b1
           + gate_ref[:, 2 * D:3 * D].astype(F32) * b2)
    o_ref[...] = h_ref[...] + mod_ref[5:6, :] * _dot(mix.astype(BF16), wout_ref[...])


def _merge(h, mods, l, mrow, attn, yssm, gqa, gate, lw):
    B, R, D = h.shape
    tm = min(512, R)
    row = lambda w: pl.BlockSpec((None, tm, w), lambda b, i: (b, i, 0))
    return pl.pallas_call(
        functools.partial(_merge_kernel, d_model=D),
        grid=(B, R // tm),
        in_specs=[row(D),
                  _mod_spec(mods, l, mrow),
                  row(_VW),
                  pl.BlockSpec((None, SSM_GROUPS, tm // CHUNK, _ZW), lambda b, i: (b, 0, i, 0)),
                  row(_GQW), row(N_BRANCH * D),
                  _layer(lw["wo"], l), _layer(lw["wglu"], l), _layer(lw["wgo"], l), _layer(lw["wout"], l)],
        out_specs=row(D),
        out_shape=jax.ShapeDtypeStruct((B, R, D), F32),
        scratch_shapes=[pltpu.VMEM((SSM_WIDTH // LANES, tm, LANES), F32)],
        compiler_params=_params(("arbitrary", "arbitrary")),
        name="merge",
    )(h, mods, attn, yssm, gqa, gate, lw["wo"], lw["wglu"], lw["wgo"], lw["wout"])


def _rope_angles(T, n):
    t = np.arange(T)
    inv = ROPE_BASE ** (-np.arange(0, n, 2, dtype=np.float32) / n)
    out = []
    for pos in (t // GRID_W, t % GRID_W):
        ang = jnp.asarray(pos.astype(np.float32))[:, None] * jnp.asarray(inv)[None, :]
        out.append((jnp.cos(ang), jnp.sin(ang)))
    cos = jnp.concatenate([out[0][0], out[0][0], out[1][0], out[1][0]], axis=1)
    sin = jnp.concatenate([-out[0][1], out[0][1], -out[1][1], out[1][1]], axis=1)
    return cos, sin


def _tables(T, C):
    sc_m = (MLA_NOPE + MLA_ROPE) ** -0.5 * LOG2E
    sc_g = GQA_HEAD_DIM ** -0.5 * LOG2E
    cm, sm = _rope_angles(T, MLA_ROPE // 2)
    cg, sg = _rope_angles(T, GQA_HEAD_DIM // 2)
    one = lambda n, w: jnp.ones((n, w), F32)
    zero = lambda n, w: jnp.zeros((n, w), F32)
    pad = LANES - MLA_NOPE - MLA_ROPE

    def mla_tab(n, c, s):
        cos_q = jnp.concatenate([one(n, MLA_NOPE), c, zero(n, pad)], axis=1) * sc_m
        sin_q = jnp.concatenate([zero(n, MLA_NOPE), s, zero(n, pad)], axis=1) * sc_m
        kr_cos = jnp.concatenate([c, zero(n, LANES - MLA_ROPE)], axis=1)
        kr_sin = jnp.concatenate([s, zero(n, LANES - MLA_ROPE)], axis=1)
        return jnp.concatenate([cos_q, sin_q, kr_cos, kr_sin], axis=1)

    def gqa_tab(n, c, s):
        c2, s2 = jnp.concatenate([c, c], axis=1), jnp.concatenate([s, s], axis=1)
        return jnp.concatenate([c2 * sc_g, s2 * sc_g, c2, s2], axis=1)

    lat = (mla_tab(T, cm, sm), gqa_tab(T, cg, sg))
    ctx = (mla_tab(C, one(C, MLA_ROPE), zero(C, MLA_ROPE)),
           gqa_tab(C, one(C, GQA_HEAD_DIM), zero(C, GQA_HEAD_DIM)))
    return lat, ctx


def _place_matrix():
    e = np.zeros((LANES, _QW), np.float32)
    for h in range(MLA_HEADS):
        for j in range(MLA_ROPE):
            e[j, h * HEAD_BLK + MLA_NOPE + j] = 1.0
    return jnp.asarray(e, BF16)


def _layer_weights(w_in, q_norm, kv_norm, w_uq, w_ukv, w_o, w_glu, gqa_w_o, w_out):
    D = w_in.shape[0]
    offs = np.cumsum([MLA_Q_RANK, MLA_KV_RANK, MLA_ROPE, SSM_WIDTH, _GQW, _GKW, _GKW]).tolist()
    cq, ckv, kr, u, gq, gk, gv, gates = jnp.split(w_in, offs, axis=1)
    order = np.asarray(GQA_ORDER)
    gq_h = gq.reshape(D, GQA_HEADS, GQA_HEAD_DIM)
    kr_blk = jnp.concatenate([kr, jnp.zeros((D, LANES - MLA_ROPE), F32)], axis=1)
    w1 = jnp.concatenate([cq, gv, ckv, u, gq_h[:, order].reshape(D, _GQW), gk, kr_blk, gates], axis=1).astype(BF16)
    uq = w_uq.reshape(MLA_Q_RANK, MLA_HEADS, MLA_NOPE + MLA_ROPE)
    zpad = jnp.zeros((MLA_Q_RANK, MLA_HEADS, HEAD_BLK - MLA_NOPE - MLA_ROPE), F32)
    wq = jnp.concatenate([uq, zpad], axis=-1).reshape(MLA_Q_RANK, _QW).astype(BF16)
    ukv = w_ukv.reshape(MLA_KV_RANK, MLA_HEADS, MLA_NOPE + MLA_V)
    wk = jnp.concatenate([ukv[..., :MLA_NOPE], jnp.zeros((MLA_KV_RANK, MLA_HEADS, HEAD_BLK - MLA_NOPE), F32)],
                         axis=-1).reshape(MLA_KV_RANK, _QW)
    wv = ukv[..., MLA_NOPE:].reshape(MLA_KV_RANK, _VW)
    wkv = jnp.concatenate([wk, wv], axis=1).astype(BF16)
    wgo = gqa_w_o.reshape(GQA_HEADS, GQA_HEAD_DIM, D)[order].reshape(_GQW, D).astype(BF16)
    return dict(w1=w1, qn=q_norm.reshape(1, -1), kvn=kv_norm.reshape(1, -1), wq=wq, wkv=wkv,
                wo=w_o.astype(BF16), wglu=w_glu.astype(BF16), wgo=wgo, wout=w_out.astype(BF16))


def _ssm_weights(lam_re, lam_im, log_dt, b_re, b_im, c_re, c_im, d_skip):
    G, P, M, Lc = SSM_GROUPS, SSM_STATE, SSM_GROUP, CHUNK
    dt = jnp.exp(log_dt)[..., None]
    kk = jnp.arange(Lc + 1, dtype=F32)[:, None, None, None]
    mag = jnp.exp(lam_re[None] * dt[None] * kk)
    pw_re, pw_im = mag * jnp.cos(lam_im[None] * dt[None] * kk), mag * jnp.sin(lam_im[None] * dt[None] * kk)
    a_re, a_im = pw_re[1], pw_im[1]
    den = lam_re * lam_re + lam_im * lam_im
    w_re = ((a_re - 1) * lam_re + a_im * lam_im) / den
    w_im = (a_im * lam_re - (a_re - 1) * lam_im) / den
    bb_re = w_re[..., None] * b_re - w_im[..., None] * b_im
    bb_im = w_re[..., None] * b_im + w_im[..., None] * b_re
    ca_re = c_re[:, None] * jnp.moveaxis(pw_re, 0, 1)[:, :, :, None, :] - c_im[:, None] * jnp.moveaxis(pw_im, 0, 1)[:, :, :, None, :]
    ca_im = c_re[:, None] * jnp.moveaxis(pw_im, 0, 1)[:, :, :, None, :] + c_im[:, None] * jnp.moveaxis(pw_re, 0, 1)[:, :, :, None, :]
    cb = jnp.sum(ca_re[..., None] * bb_re[:, None, :, None] - ca_im[..., None] * bb_im[:, None, :, None], axis=-2)
    lags_f = jnp.transpose(cb[0, :Lc], (1, 3, 0, 2)).reshape(G, M, Lc * M)
    lags_r = jnp.transpose(cb[1, :Lc][::-1], (1, 3, 0, 2)).reshape(G, M, Lc * M)
    halo = jnp.zeros((G, M, (Lc - 1) * M), F32)
    ext = jnp.concatenate([halo, lags_f], axis=-1) + jnp.concatenate([lags_r, halo], axis=-1)
    kmat = jnp.stack([ext[:, :, (Lc - 1 - i) * M:(Lc - 1 - i) * M + Lc * M] for i in range(Lc)], axis=1)
    skip = jnp.tile(d_skip.reshape(G, M), (1, Lc))
    kmat = kmat.reshape(G, Lc * M, Lc * M) + jnp.eye(Lc * M, dtype=F32)[None] * skip[:, None, :]
    idx_f = np.arange(Lc)[::-1].copy()
    idx_r = np.arange(Lc)

    def drive(d, idx):
        p_re, p_im = pw_re[idx, d], pw_im[idx, d]
        e_re = p_re[..., None] * bb_re[d][None] - p_im[..., None] * bb_im[d][None]
        e_im = p_re[..., None] * bb_im[d][None] + p_im[..., None] * bb_re[d][None]
        to = lambda t: jnp.transpose(t, (1, 0, 3, 2)).reshape(G, Lc * M, P)
        return to(e_re), to(e_im)

    ef_re, ef_im = drive(0, idx_f)
    er_re, er_im = drive(1, idx_r)

    def readout(d, idx):
        q_re, q_im = ca_re[d][idx], ca_im[d][idx]
        to = lambda t: jnp.transpose(t, (1, 3, 0, 2)).reshape(G, P, Lc * M)
        return to(q_re), to(-q_im)

    qf_re, qf_im = readout(0, np.arange(1, Lc + 1))
    qr_re, qr_im = readout(1, Lc - np.arange(Lc))
    npair = G // PAIR

    def pair_diag(t):
        R, Cn = t.shape[1:]
        t = t.reshape(npair, PAIR, R, Cn)
        z = jnp.zeros((npair, R, Cn), F32)
        rows = [jnp.concatenate([t[:, p] if q == p else z for q in range(PAIR)], axis=-1) for p in range(PAIR)]
        return jnp.concatenate(rows, axis=1)

    we = jnp.concatenate([pair_diag(ef_re), pair_diag(ef_im), pair_diag(er_re), pair_diag(er_im)], axis=2)
    wy = jnp.concatenate([pair_diag(kmat), pair_diag(qf_re), pair_diag(qf_im), pair_diag(qr_re), pair_diag(qr_im)], axis=1)
    sw = PAIR * P
    nstep = npair // PAIR

    def per_row(t):
        t = t.reshape(nstep, PAIR, 1, sw)
        return jnp.broadcast_to(t, (nstep, PAIR, SUBLANES // PAIR, sw)).reshape(nstep, SUBLANES, sw)

    a_step = jnp.stack([per_row(pw_re[Lc, 0]), per_row(pw_im[Lc, 0]), per_row(pw_re[Lc, 1]), per_row(pw_im[Lc, 1])],
                       axis=1)
    return we.astype(BF16), wy.astype(BF16), a_step


def kernel(x, c, ctx, c_ctx, ada_w, ada_b, norm_ffn1, norm_mix, norm_ffn2, ffn1_w13, ffn1_w2, ffn2_w13, ffn2_w2, w_in, mla_q_norm, mla_kv_norm, mla_w_uq, mla_w_ukv, mla_w_o, ssm_lambda_re, ssm_lambda_im, ssm_log_dt, ssm_b_re, ssm_b_im, ssm_c_re, ssm_c_im, ssm_d, ssm_w_glu, gqa_sink, gqa_w_o, w_out, final_norm):
    B, T, D = x.shape
    C = ctx.shape[1]
    depth = ada_w.shape[0]
    cc = jnp.concatenate([c, c_ctx[None], jnp.zeros((SUBLANES - B - 1, D), F32)], axis=0)
    mods = _ada(cc, ada_w, ada_b).reshape(depth, SUBLANES, N_MOD, D)
    (tabm, tabg), (tabm_c, tabg_c) = _tables(T, C)
    sinks = gqa_sink * LOG2E
    lw = jax.vmap(_layer_weights)(w_in, mla_q_norm, mla_kv_norm, mla_w_uq, mla_w_ukv, mla_w_o, ssm_w_glu, gqa_w_o,
                                  w_out)
    we, wy, a_step = jax.vmap(_ssm_weights)(ssm_lambda_re, ssm_lambda_im, ssm_log_dt, ssm_b_re, ssm_b_im,
                                            ssm_c_re, ssm_c_im, ssm_d)
    e_mat = _place_matrix()
    ffn1 = (norm_ffn1[:, None, :], ffn1_w13.astype(BF16), ffn1_w2.astype(BF16))
    ffn2 = (norm_ffn2[:, None, :], ffn2_w13.astype(BF16), ffn2_w2.astype(BF16))
    g_mix = norm_mix[:, None, :]
    lat, cx = None, B
    h, hc = x, ctx
    for l in range(depth):
        ctx_out = l < depth - 1
        last = l == depth - 1
        h = _ffn(h, mods, l, lat, *ffn1, 0)
        hc = _ffn(hc, mods, l, cx, *ffn1, 0)
        q, k, v, z, gq, gk, gv, gate = _inproj(h, mods, l, lat, g_mix, lw, e_mat, tabm, tabg)
        q_c, k_c, v_c, z_c, gq_c, gk_c, gv_c, gate_c = _inproj(hc, mods, l, cx, g_mix, lw, e_mat, tabm_c, tabg_c)
        attn = _mla(q, k_c, v_c, k, v)
        y_lat, y_ctx = _ssm(z, z_c, we, wy, a_step, l)
        gqa = _gqa(sinks, l, gq, gk_c, gv_c, gk, gv)
        h = _merge(h, mods, l, lat, attn, y_lat, gqa, gate, lw)
        h = _ffn(h, mods, l, lat, *ffn2, 6, final_norm if last else None)
        if ctx_out:
            attn_c = _mla(q_c, k_c, v_c)
            gqa_c = _gqa(sinks, l, gq_c, gk_c, gv_c)
            hc = _merge(hc, mods, l, cx, attn_c, y_ctx, gqa_c, gate_c, lw)
            hc = _ffn(hc, mods, l, cx, *ffn2, 6)
    return h
```
